```python
import math
import jax
import jax.numpy as jnp
from jax import lax
import numpy as np

D_MODEL = 4096
BATCH = 4
SEQ = 2048
DEPTH = 2
DEC_BATCH = 8
DEC_SEQ = 1
PAST_LEN = 16384
PAGE_SIZE = 128

HEAD_DIM = 128
MOBA_HEADS = D_MODEL // (2 * HEAD_DIM)
DIFF_HEADS = D_MODEL // (4 * HEAD_DIM)
SB_HEADS = D_MODEL // HEAD_DIM
MOBA_W = MOBA_HEADS * HEAD_DIM
DIFF_W = DIFF_HEADS * 2 * HEAD_DIM
SB_W = SB_HEADS * HEAD_DIM
MOBA_BLOCK = 256
MOBA_TOPK = 3
MOBA_QCHUNK = 16
Q_BLOCK = 128
NUM_BUCKETS = 32
MAX_DISTANCE = 128
BIAS_HEADS = MOBA_HEADS + DIFF_HEADS
D_FF = ((8 * D_MODEL // 3 + 255) // 256) * 256
CONV_W = 3
PLE_DIM = 256
N_EVEN = (DEPTH + 1) // 2
N_ODD = DEPTH // 2
EPS = 1e-6

kernel_name = 'moba_diff_stickbreak_convffn_decode_step'


def rmsnorm(x, g):
    xf = x.astype(jnp.float32)
    y = xf * lax.rsqrt(jnp.mean(xf * xf, axis=-1, keepdims=True) + EPS)
    return (y * g.astype(jnp.float32)).astype(x.dtype)


def rel_bucket(dist):
    n = jnp.maximum(dist, 0)
    max_exact = NUM_BUCKETS // 2
    nf = jnp.maximum(n, 1).astype(jnp.float32)
    large = max_exact + (jnp.log(nf / max_exact) / math.log(MAX_DISTANCE / max_exact)
                         * (NUM_BUCKETS - max_exact)).astype(jnp.int32)
    return jnp.where(n < max_exact, n, jnp.minimum(large, NUM_BUCKETS - 1))


def moba_attention(q, k, v, q_start, bias_t):
    b, tq, h, dh = q.shape
    l = k.shape[1]
    nb = -(-l // MOBA_BLOCK)
    pad = nb * MOBA_BLOCK - l

    def to_blocks(a):
        a = jnp.pad(a, ((0, 0), (0, pad), (0, 0), (0, 0)))
        return a.reshape(b, nb, MOBA_BLOCK, h, dh).transpose(0, 3, 1, 2, 4)

    kb, vb = to_blocks(k), to_blocks(v)
    kmean = jnp.mean(kb.astype(jnp.float32), axis=3)
    n_sel = min(MOBA_TOPK, nb)
    qc = MOBA_QCHUNK if tq % MOBA_QCHUNK == 0 else tq
    nc = tq // qc
    scale = dh ** -0.5
    bi = jnp.arange(b)[:, None, None, None]
    hi = jnp.arange(h)[None, None, :, None]
    jj = jnp.arange(MOBA_BLOCK)

    def chunk(args):
        qi, ci = args
        t = q_start + ci * qc + jnp.arange(qc)
        bt = t // MOBA_BLOCK
        qf = qi.astype(jnp.float32)
        gate = jnp.einsum('bqhd,bhnd->bqhn', qf, kmean)
        fully_past = jnp.arange(nb)[None, :] < bt[:, None]
        gate = jnp.where(fully_past[None, :, None, :], gate, -jnp.inf)
        _, idx = lax.top_k(gate, n_sel)
        valid = idx < bt[None, :, None, None]
        k_sel = kb[bi, hi, idx].astype(jnp.float32)
        v_sel = vb[bi, hi, idx].astype(jnp.float32)
        s_past = jnp.einsum('bqhd,bqhnjd->bqhnj', qf, k_sel) * scale
        pos_past = idx[..., None] * MOBA_BLOCK + jj
        b_past = bias_t[hi[..., None], rel_bucket(t[None, :, None, None, None] - pos_past)]
        s_past = jnp.where(valid[..., None], s_past + b_past, -jnp.inf)
        k_own = kb[:, :, bt].astype(jnp.float32)
        v_own = vb[:, :, bt].astype(jnp.float32)
        d_own = t[:, None] - (bt[:, None] * MOBA_BLOCK + jj)
        s_own = (jnp.einsum('bqhd,bhqjd->bqhj', qf, k_own) * scale
                 + bias_t[:, rel_bucket(d_own)].transpose(1, 0, 2)[None])
        s_own = jnp.where((d_own >= 0)[None, :, None, :], s_own, -jnp.inf)
        logits = jnp.concatenate([s_past.reshape(b, qc, h, n_sel * MOBA_BLOCK), s_own], axis=-1)
        w = jax.nn.softmax(logits, axis=-1)
        w_past = w[..., : n_sel * MOBA_BLOCK].reshape(b, qc, h, n_sel, MOBA_BLOCK)
        w_own = w[..., n_sel * MOBA_BLOCK:]
        o = (jnp.einsum('bqhnj,bqhnjd->bqhd', w_past, v_sel)
             + jnp.einsum('bqhj,bhqjd->bqhd', w_own, v_own))
        return o.astype(q.dtype)

    qs = q.reshape(b, nc, qc, h, dh).transpose(1, 0, 2, 3, 4)
    o = lax.map(chunk, (qs, jnp.arange(nc)))
    return o.transpose(1, 0, 2, 3, 4).reshape(b, tq, h, dh)


def diff_attention(q, k, v, q_start, bias_t, lam, lam_init, sub_g):
    b, tq, h, _, dh = q.shape
    l = k.shape[1]
    qb = Q_BLOCK if tq % Q_BLOCK == 0 else tq
    nq = tq // qb
    scale = dh ** -0.5
    kf, vf = k.astype(jnp.float32), v.astype(jnp.float32)
    kpos = jnp.arange(l)
    g = sub_g.astype(jnp.float32) * (1.0 - lam_init)

    def block(args):
        qi, ci = args
        t = q_start + ci * qb + jnp.arange(qb)
        dist = t[:, None] - kpos[None, :]
        s = (jnp.einsum('bqhcd,bkhcd->bhcqk', qi.astype(jnp.float32), kf) * scale
             + bias_t[:, rel_bucket(dist)][None, :, None])
        pr = jax.nn.softmax(jnp.where(dist >= 0, s, -jnp.inf), axis=-1)
        a = pr[:, :, 0] - lam * pr[:, :, 1]
        o = jnp.einsum('bhqk,bkhe->bqhe', a, vf)
        o = o * lax.rsqrt(jnp.mean(o * o, axis=-1, keepdims=True) + EPS) * g
        return o.astype(q.dtype)

    qs = q.reshape(b, nq, qb, h, 2, dh).transpose(1, 0, 2, 3, 4, 5)
    o = lax.map(block, (qs, jnp.arange(nq)))
    return o.transpose(1, 0, 2, 3, 4).reshape(b, tq, h, 2 * dh)


def stick_breaking_attention(q, k, v, q_start):
    b, tq, h, dh = q.shape
    l = k.shape[1]
    qb = Q_BLOCK if tq % Q_BLOCK == 0 else tq
    nq = tq // qb
    scale = dh ** -0.5
    kf, vf = k.astype(jnp.float32), v.astype(jnp.float32)
    kpos = jnp.arange(l)

    def block(args):
        qi, ci = args
        t = q_start + ci * qb + jnp.arange(qb)
        past = kpos[None, :] < t[:, None]
        z = jnp.einsum('bqhd,bkhd->bhqk', qi.astype(jnp.float32), kf) * scale
        log_keep = jnp.where(past, jax.nn.log_sigmoid(-z), 0.0)
        between = lax.cumsum(log_keep, axis=3, reverse=True) - log_keep
        a = jnp.where(past, jnp.exp(jax.nn.log_sigmoid(z) + between), 0.0)
        o = jnp.einsum('bhqk,bkhd->bqhd', a, vf)
        return o.astype(q.dtype)

    qs = q.reshape(b, nq, qb, h, dh).transpose(1, 0, 2, 3, 4)
    o = lax.map(block, (qs, jnp.arange(nq)))
    return o.transpose(1, 0, 2, 3, 4).reshape(b, tq, h, dh)


def over_paged_past(attn, q, k_new, v_new, cache_k, cache_v, layer, page_table):
    past_len = page_table.shape[1] * PAGE_SIZE

    def one(args):
        qs, kn, vn, pt = args
        kp = cache_k[layer, pt].reshape((past_len,) + kn.shape[1:]).astype(kn.dtype)
        vp = cache_v[layer, pt].reshape((past_len,) + vn.shape[1:]).astype(vn.dtype)
        k_all = jnp.concatenate([kp, kn], axis=0)[None]
        v_all = jnp.concatenate([vp, vn], axis=0)[None]
        return attn(qs[None], k_all, v_all, past_len)[0]

    return lax.map(one, (q, k_new, v_new, page_table))


def even_mixer(a, i, w, caches, page_table):
    j = i // 2
    b, t, _ = a.shape
    proj = a @ w['w_in_even'][j]
    mq, mk, mv, dq, dk, dv = jnp.split(
        proj, [MOBA_W, 2 * MOBA_W, 3 * MOBA_W, 3 * MOBA_W + DIFF_W, 3 * MOBA_W + 2 * DIFF_W], axis=-1)
    mq = mq.reshape(b, t, MOBA_HEADS, HEAD_DIM)
    mk = mk.reshape(b, t, MOBA_HEADS, HEAD_DIM)
    mv = mv.reshape(b, t, MOBA_HEADS, HEAD_DIM)
    dq = dq.reshape(b, t, DIFF_HEADS, 2, HEAD_DIM)
    dk = dk.reshape(b, t, DIFF_HEADS, 2, HEAD_DIM)
    dv = dv.reshape(b, t, DIFF_HEADS, 2 * HEAD_DIM)
    bias_m = w['rel_bias'][:, :MOBA_HEADS].T.astype(jnp.float32)
    bias_d = w['rel_bias'][:, MOBA_HEADS:].T.astype(jnp.float32)
    lam_init = 0.8 - 0.6 * math.exp(-0.3 * i)
    f32 = jnp.float32
    lam = (jnp.exp(jnp.sum(w['lam_q1'][j].astype(f32) * w['lam_k1'][j].astype(f32)))
           - jnp.exp(jnp.sum(w['lam_q2'][j].astype(f32) * w['lam_k2'][j].astype(f32))) + lam_init)
    sub_g = w['diff_subln'][j]
    moba_fn = lambda q_, k_, v_, s_: moba_attention(q_, k_, v_, s_, bias_m)
    diff_fn = lambda q_, k_, v_, s_: diff_attention(q_, k_, v_, s_, bias_d, lam, lam_init, sub_g)
    if caches is None:
        mo = moba_fn(mq, mk, mv, 0)
        do = diff_fn(dq, dk, dv, 0)
    else:
        mo = over_paged_past(moba_fn, mq, mk, mv, caches['moba_k'], caches['moba_v'], j, page_table)
        do = over_paged_past(diff_fn, dq, dk, dv, caches['diff_k'], caches['diff_v'], j, page_table)
    out = jnp.concatenate([mo.reshape(b, t, MOBA_W), do.reshape(b, t, DIFF_W)], axis=-1) @ w['w_out_even'][j]
    return out, (mk, mv, dk, dv)


def odd_mixer(a, i, w, caches, page_table):
    j = i // 2
    b, t, _ = a.shape
    q, k, v = jnp.split(a @ w['w_in_odd'][j], 3, axis=-1)
    q = q.reshape(b, t, SB_HEADS, HEAD_DIM)
    k = k.reshape(b, t, SB_HEADS, HEAD_DIM)
    v = v.reshape(b, t, SB_HEADS, HEAD_DIM)
    if caches is None:
        o = stick_breaking_attention(q, k, v, 0)
    else:
        o = over_paged_past(stick_breaking_attention, q, k, v, caches['sb_k'], caches['sb_v'], j, page_table)
    return o.reshape(b, t, SB_W) @ w['w_out_odd'][j], (k, v)


def conv_ffn(f, state, w_up, conv_w, conv_b, w_down):
    t = f.shape[1]
    g, u = jnp.split(f @ w_up, 2, axis=-1)
    gx = jnp.concatenate([state.astype(g.dtype), g], axis=1)
    gc = conv_b
    for c in range(CONV_W):
        gc = gc + conv_w[c] * gx[:, c:c + t]
    return (jax.nn.silu(gc) * u) @ w_down, gx[:, t:]


def trunk(x, p, conv_state, caches, page_table, w):
    h = x
    rows = {'moba_k': [], 'moba_v': [], 'diff_k': [], 'diff_v': [], 'sb_k': [], 'sb_v': []}
    new_conv = []
    for i in range(DEPTH):
        a = rmsnorm(h, w['g_mix'][i])
        if i % 2 == 0:
            mix, (mk, mv, dk, dv) = even_mixer(a, i, w, caches, page_table)
            rows['moba_k'].append(mk)
            rows['moba_v'].append(mv)
            rows['diff_k'].append(dk)
            rows['diff_v'].append(dv)
        else:
            mix, (sk, sv) = odd_mixer(a, i, w, caches, page_table)
            rows['sb_k'].append(sk)
            rows['sb_v'].append(sv)
        h = h + mix
        f, st = conv_ffn(rmsnorm(h, w['g_ffn'][i]), conv_state[i], w['w_up'][i],
                         w['conv_w'][i], w['conv_b'][i], w['w_down'][i])
        new_conv.append(st)
        h = h + f
        gate = jax.nn.sigmoid(rmsnorm(h, w['g_ple'][i]) @ w['w_ple_gate'][i])
        h = h + gate * (p[i] @ w['w_ple_proj'][i])
    y = rmsnorm(h, w['g_final'])
    stacked = {n: jnp.stack(r) for n, r in rows.items()}
    return y, stacked, jnp.stack(new_conv)


def setup_inputs(seed: int = 0) -> dict:
    key = jax.random.key(seed)
    keys = jax.random.split(key, 48)
    counter = [0]

    def nrm(shape, scale):
        counter[0] += 1
        return jax.random.normal(keys[counter[0] - 1], shape, jnp.float32) * scale

    n_pages = PAST_LEN // PAGE_SIZE
    n_pool = (5 * DEC_BATCH * n_pages) // 4
    d = D_MODEL
    out = {}
    out['x_prompt'] = nrm((BATCH, SEQ, d), 1.0)
    out['x_sample'] = nrm((DEC_BATCH, DEC_SEQ, d), 1.0)
    out['cache_moba_k'] = nrm((N_EVEN, n_pool, PAGE_SIZE, MOBA_HEADS, HEAD_DIM), 1.0)
    out['cache_moba_v'] = nrm((N_EVEN, n_pool, PAGE_SIZE, MOBA_HEADS, HEAD_DIM), 1.0)
    out['cache_diff_k'] = nrm((N_EVEN, n_pool, PAGE_SIZE, DIFF_HEADS, 2, HEAD_DIM), 1.0)
    out['cache_diff_v'] = nrm((N_EVEN, n_pool, PAGE_SIZE, DIFF_HEADS, 2 * HEAD_DIM), 1.0)
    out['cache_sb_k'] = nrm((N_ODD, n_pool, PAGE_SIZE, SB_HEADS, HEAD_DIM), 1.0)
    out['cache_sb_v'] = nrm((N_ODD, n_pool, PAGE_SIZE, SB_HEADS, HEAD_DIM), 1.0)
    out['state_conv'] = nrm((DEPTH, DEC_BATCH, CONV_W - 1, D_FF), 1.0)
    counter[0] += 1
    perm = jax.random.permutation(keys[counter[0] - 1], n_pool)
    out['page_table'] = perm[: DEC_BATCH * n_pages].reshape(DEC_BATCH, n_pages).astype(jnp.int32)
    out['p_prompt'] = nrm((DEPTH, BATCH, SEQ, PLE_DIM), 1.0)
    out['p_sample'] = nrm((DEPTH, DEC_BATCH, DEC_SEQ, PLE_DIM), 1.0)
    out['rel_bias'] = nrm((NUM_BUCKETS, BIAS_HEADS), 0.1)
    out['w_in_even'] = nrm((N_EVEN, d, 3 * MOBA_W + 3 * DIFF_W), d ** -0.5)
    out['w_out_even'] = nrm((N_EVEN, MOBA_W + DIFF_W, d), (MOBA_W + DIFF_W) ** -0.5)
    out['lam_q1'] = nrm((N_EVEN, HEAD_DIM), 0.1)
    out['lam_k1'] = nrm((N_EVEN, HEAD_DIM), 0.1)
    out['lam_q2'] = nrm((N_EVEN, HEAD_DIM), 0.1)
    out['lam_k2'] = nrm((N_EVEN, HEAD_DIM), 0.1)
    out['diff_subln'] = 1.0 + nrm((N_EVEN, 2 * HEAD_DIM), 0.02)
    out['w_in_odd'] = nrm((N_ODD, d, 3 * SB_W), d ** -0.5)
    out['w_out_odd'] = nrm((N_ODD, SB_W, d), SB_W ** -0.5)
    out['g_mix'] = 1.0 + nrm((DEPTH, d), 0.02)
    out['g_ffn'] = 1.0 + nrm((DEPTH, d), 0.02)
    out['w_up'] = nrm((DEPTH, d, 2 * D_FF), d ** -0.5)
    out['conv_w'] = nrm((DEPTH, CONV_W, D_FF), CONV_W ** -0.5)
    out['conv_b'] = nrm((DEPTH, D_FF), 0.01)
    out['w_down'] = nrm((DEPTH, D_FF, d), D_FF ** -0.5)
    out['g_ple'] = 1.0 + nrm((DEPTH, d), 0.02)
    out['w_ple_gate'] = nrm((DEPTH, d, d), d ** -0.5)
    out['w_ple_proj'] = nrm((DEPTH, PLE_DIM, d), PLE_DIM ** -0.5)
    out['g_final'] = 1.0 + nrm((d,), 0.02)
    return out


def reference(x_prompt, x_sample, cache_moba_k, cache_moba_v, cache_diff_k, cache_diff_v,
              cache_sb_k, cache_sb_v, state_conv, page_table, p_prompt, p_sample,
              rel_bias, w_in_even, w_out_even, lam_q1, lam_k1, lam_q2, lam_k2, diff_subln,
              w_in_odd, w_out_odd, g_mix, g_ffn, w_up, conv_w, conv_b, w_down,
              g_ple, w_ple_gate, w_ple_proj, g_final):
    w = {'rel_bias': rel_bias, 'w_in_even': w_in_even, 'w_out_even': w_out_even,
         'lam_q1': lam_q1, 'lam_k1': lam_k1, 'lam_q2': lam_q2, 'lam_k2': lam_k2,
         'diff_subln': diff_subln, 'w_in_odd': w_in_odd, 'w_out_odd': w_out_odd,
         'g_mix': g_mix, 'g_ffn': g_ffn, 'w_up': w_up, 'conv_w': conv_w, 'conv_b': conv_b,
         'w_down': w_down, 'g_ple': g_ple, 'w_ple_gate': w_ple_gate,
         'w_ple_proj': w_ple_proj, 'g_final': g_final}
    caches = {'moba_k': cache_moba_k, 'moba_v': cache_moba_v, 'diff_k': cache_diff_k,
              'diff_v': cache_diff_v, 'sb_k': cache_sb_k, 'sb_v': cache_sb_v}
    conv0 = jnp.zeros((DEPTH, x_prompt.shape[0], CONV_W - 1, D_FF), x_prompt.dtype)
    y_prompt, rp, conv_prompt = trunk(x_prompt, p_prompt, conv0, None, None, w)
    y_sample, rs, conv_sample = trunk(x_sample, p_sample, state_conv, caches, page_table, w)
    return (y_prompt, y_sample,
            rp['moba_k'], rp['moba_v'], rp['diff_k'], rp['diff_v'], rp['sb_k'], rp['sb_v'], conv_prompt,
            rs['moba_k'], rs['moba_v'], rs['diff_k'], rs['diff_v'], rs['sb_k'], rs['sb_v'], conv_sample)
```

```python
import functools
import math

import numpy as np
import jax
import jax.numpy as jnp
from jax import lax
from jax.experimental import pallas as pl
from jax.experimental.pallas import tpu as pltpu

F32 = jnp.float32
BF16 = jnp.bfloat16

HEAD_DIM = 128
MOBA_BLOCK = 256
MOBA_TOPK = 3
NUM_BUCKETS = 32
MAX_DISTANCE = 128
PAGE_SIZE = 128
CONV_W = 3
EPS = 1e-6
NEG = -1e30
ATT_TILE = 256
SB_TILE = 128
V7X_VMEM_BYTES = 64 * 1024 * 1024
VMEM_CAP = V7X_VMEM_BYTES - 8 * 1024 * 1024
NT_DIMS = (((1,), (1,)), ((), ()))


def _params(sem, vmem_bytes):
    limit = int(min(max(vmem_bytes * 5 // 4, 16 * 1024 * 1024), VMEM_CAP))
    return pltpu.CompilerParams(dimension_semantics=sem, vmem_limit_bytes=limit)


def _bucket_np(dist):
    n = np.maximum(dist, 0)
    max_exact = NUM_BUCKETS // 2
    nf = np.maximum(n, 1).astype(np.float32)
    large = max_exact + (np.log(nf / np.float32(max_exact)) / np.float32(math.log(MAX_DISTANCE / max_exact))
                         * np.float32(NUM_BUCKETS - max_exact)).astype(np.int32)
    return np.where(n < max_exact, n, np.minimum(large, NUM_BUCKETS - 1)).astype(np.int32)


def _bias_table_kernel(rb_ref, bkt_ref, mask_ref, o_ref):
    h = pl.program_id(0)
    bkt = bkt_ref[...]
    acc = jnp.zeros(bkt.shape, F32)
    for b in range(NUM_BUCKETS):
        acc = jnp.where(bkt == b, rb_ref[h * NUM_BUCKETS + b], acc)
    o_ref[...] = acc + mask_ref[...]


def _bias_table(rb_flat, bkt, mask):
    n_heads = rb_flat.shape[0] // NUM_BUCKETS
    r, c = bkt.shape
    return pl.pallas_call(
        _bias_table_kernel,
        grid_spec=pltpu.PrefetchScalarGridSpec(
            num_scalar_prefetch=1, grid=(n_heads,),
            in_specs=[pl.BlockSpec((r, c), lambda h, rb: (0, 0)),
                      pl.BlockSpec((r, c), lambda h, rb: (0, 0))],
            out_specs=pl.BlockSpec((None, r, c), lambda h, rb: (h, 0, 0))),
        out_shape=jax.ShapeDtypeStruct((n_heads, r, c), F32),
        compiler_params=_params(("arbitrary",), 8 * r * c * 4),
    )(rb_flat, jnp.asarray(bkt), jnp.asarray(mask))


def _prompt_bias_tables(rb_flat):
    t = ATT_TILE
    r = np.arange(t)[:, None]
    c = np.arange(t)[None, :]
    d0 = r - c
    d1 = t + r - c
    assert int(_bucket_np(np.array([2 * t - (t - 1)]))[0]) == NUM_BUCKETS - 1
    bkt = np.concatenate([_bucket_np(d0), _bucket_np(d1)], axis=0)
    mask = np.concatenate([np.where(d0 >= 0, 0.0, NEG), np.zeros((t, t))], axis=0).astype(np.float32)
    return _bias_table(rb_flat, bkt, mask)


def _sample_bias_rows(rb_flat):
    assert int(_bucket_np(np.array([PAGE_SIZE + 1]))[0]) == NUM_BUCKETS - 1
    j = np.arange(PAGE_SIZE)
    bkt = np.zeros((8, PAGE_SIZE), np.int32)
    mask = np.zeros((8, PAGE_SIZE), np.float32)
    bkt[0] = NUM_BUCKETS - 1
    bkt[1] = _bucket_np(PAGE_SIZE - j)
    bkt[2] = 0
    mask[2, 1:] = NEG
    return _bias_table(rb_flat, bkt, mask)


def _rmsnorm_kernel(x_ref, g_ref, o_ref):
    x = x_ref[...]
    y = x * lax.rsqrt(jnp.mean(x * x, axis=-1, keepdims=True) + EPS)
    o_ref[...] = (y * g_ref[...]).astype(o_ref.dtype)


def _rmsnorm(x, g, out_dtype):
    m, d = x.shape
    tr = min(m, 256)
    assert m % tr == 0
    return pl.pallas_call(
        _rmsnorm_kernel,
        grid=(m // tr,),
        in_specs=[pl.BlockSpec((tr, d), lambda i: (i, 0)),
                  pl.BlockSpec((1, d), lambda i: (0, 0))],
        out_specs=pl.BlockSpec((tr, d), lambda i: (i, 0)),
        out_shape=jax.ShapeDtypeStruct((m, d), out_dtype),
        compiler_params=_params(("parallel",), 4 * tr * d * 4),
    )(x, g.reshape(1, d))


def _mm_kernel(*refs, nk, n_extra, epilogue):
    a_ref, w_ref = refs[0], refs[1]
    extra = refs[2:2 + n_extra]
    o_ref = refs[2 + n_extra]

    def finish(acc):
        o_ref[...] = epilogue(acc, *[e[...] for e in extra]).astype(o_ref.dtype)

    if nk == 1:
        finish(jnp.dot(a_ref[...], w_ref[...], preferred_element_type=F32))
    else:
        acc_ref = refs[3 + n_extra]
        k = pl.program_id(2)

        @pl.when(k == 0)
        def _():
            acc_ref[...] = jnp.zeros_like(acc_ref)

        acc_ref[...] += jnp.dot(a_ref[...], w_ref[...], preferred_element_type=F32)

        @pl.when(k == nk - 1)
        def _():
            finish(acc_ref[...])


def _mm(a, w, *, tm, tn, tk, extras=(), epilogue=None, out_dtype=F32):
    m, kdim = a.shape
    n = w.shape[1]
    tm, tn, tk = min(tm, m), min(tn, n), min(tk, kdim)
    assert m % tm == 0 and n % tn == 0 and kdim % tk == 0
    nk = kdim // tk
    if epilogue is None:
        epilogue = lambda acc: acc
    in_specs = [pl.BlockSpec((tm, tk), lambda i, j, k: (i, k)),
                pl.BlockSpec((tk, tn), lambda i, j, k: (k, j))]
    in_specs += [pl.BlockSpec((tm, tn), lambda i, j, k: (i, j)) for _ in extras]
    scratch = [pltpu.VMEM((tm, tn), F32)] if nk > 1 else []
    vmem = 2 * (tm * tk * 2 + tk * tn * 2) + (2 * len(extras) + 4) * tm * tn * 4
    return pl.pallas_call(
        functools.partial(_mm_kernel, nk=nk, n_extra=len(extras), epilogue=epilogue),
        grid=(m // tm, n // tn, nk),
        in_specs=in_specs,
        out_specs=pl.BlockSpec((tm, tn), lambda i, j, k: (i, j)),
        out_shape=jax.ShapeDtypeStruct((m, n), out_dtype),
        scratch_shapes=scratch,
        compiler_params=_params(("parallel", "parallel", "arbitrary"), vmem),
    )(a, w, *extras)


def _add_residual(acc, h):
    return h + acc


def _ple_combine(acc, h, pp):
    return h + jax.nn.sigmoid(acc) * pp


def _conv_taps(cw_ref, cb_ref, g2, g1, g0):
    gc = cb_ref[...] + cw_ref[0:1, :] * g2
    gc = gc + cw_ref[1:2, :] * g1
    return gc + cw_ref[2:3, :] * g0


def _ffn_gate_prompt_kernel(g_ref, gp_ref, u_ref, cw_ref, cb_ref, o_ref, *, seq):
    tm = g_ref.shape[0]
    g = g_ref[...]
    prev = gp_ref[...]
    row = lax.broadcasted_iota(jnp.int32, (tm, 1), 0)
    pos = (pl.program_id(0) * tm + row) % seq
    g1 = jnp.where(row == 0, prev[7:8, :], pltpu.roll(g, 1, axis=0))
    g1 = jnp.where(pos >= 1, g1, 0.0)
    g2 = jnp.where(row == 0, prev[6:7, :], jnp.where(row == 1, prev[7:8, :], pltpu.roll(g, 2, axis=0)))
    g2 = jnp.where(pos >= 2, g2, 0.0)
    gc = _conv_taps(cw_ref, cb_ref, g2, g1, g)
    o_ref[...] = (jax.nn.silu(gc) * u_ref[...]).astype(o_ref.dtype)


def _ffn_gate_prompt(gu, conv_w, conv_b, seq):
    m = gu.shape[0]
    dff = conv_w.shape[1]
    tm, tn = 512, 256
    assert m % tm == 0 and dff % tn == 0 and seq % tm == 0
    nj = dff // tn
    return pl.pallas_call(
        functools.partial(_ffn_gate_prompt_kernel, seq=seq),
        grid=(m // tm, nj),
        in_specs=[pl.BlockSpec((tm, tn), lambda i, j: (i, j)),
                  pl.BlockSpec((8, tn), lambda i, j: (jnp.maximum(i * (tm // 8) - 1, 0), j)),
                  pl.BlockSpec((tm, tn), lambda i, j: (i, j + nj)),
                  pl.BlockSpec((CONV_W, tn), lambda i, j: (0, j)),
                  pl.BlockSpec((1, tn), lambda i, j: (0, j))],
        out_specs=pl.BlockSpec((tm, tn), lambda i, j: (i, j)),
        out_shape=jax.ShapeDtypeStruct((m, dff), BF16),
        compiler_params=_params(("parallel", "parallel"), 8 * tm * tn * 4),
    )(gu, gu, gu, conv_w, conv_b.reshape(1, dff))


def _ffn_gate_sample_kernel(g_ref, u_ref, s0_ref, s1_ref, cw_ref, cb_ref, o_ref):
    gc = _conv_taps(cw_ref, cb_ref, s0_ref[...], s1_ref[...], g_ref[...])
    o_ref[...] = (jax.nn.silu(gc) * u_ref[...]).astype(o_ref.dtype)


def _ffn_gate_sample(gu, s0, s1, conv_w, conv_b):
    m = gu.shape[0]
    dff = conv_w.shape[1]
    tn = 256
    nj = dff // tn
    blk = lambda i: (0, i)
    return pl.pallas_call(
        _ffn_gate_sample_kernel,
        grid=(nj,),
        in_specs=[pl.BlockSpec((m, tn), blk),
                  pl.BlockSpec((m, tn), lambda j: (0, j + nj)),
                  pl.BlockSpec((m, tn), blk),
                  pl.BlockSpec((m, tn), blk),
                  pl.BlockSpec((CONV_W, tn), blk),
                  pl.BlockSpec((1, tn), blk)],
        out_specs=pl.BlockSpec((m, tn), blk),
        out_shape=jax.ShapeDtypeStruct((m, dff), BF16),
        compiler_params=_params(("parallel",), 16 * m * tn * 4),
    )(gu, gu, s0, s1, conv_w, conv_b.reshape(1, dff))


def _softmax_step(s, v, m_ref, l_ref, acc_ref):
    m_prev = m_ref[...]
    m_new = jnp.maximum(m_prev, jnp.max(s, axis=1, keepdims=True))
    alpha = jnp.exp(m_prev - m_new)
    p = jnp.exp(s - m_new)
    l_ref[...] = alpha * l_ref[...] + jnp.sum(p, axis=1, keepdims=True)
    acc_ref[...] = alpha * acc_ref[...] + jnp.dot(p.astype(BF16), v, preferred_element_type=F32)
    m_ref[...] = m_new


def _diff_lambda(lq1, lk1, lq2, lk2, lam_init):
    return (jnp.exp(jnp.sum(lq1[...] * lk1[...], axis=1, keepdims=True))
            - jnp.exp(jnp.sum(lq2[...] * lk2[...], axis=1, keepdims=True)) + lam_init)


def _sub_rmsnorm(o, g_ref, lam_init):
    return o * lax.rsqrt(jnp.mean(o * o, axis=-1, keepdims=True) + EPS) * (g_ref[...] * (1.0 - lam_init))


def _moba_prompt_kernel(rb_ref, q_ref, k_ref, v_ref, bias_ref, o_ref,
                        kb_sc, vb_sc, km_sc, m_sc, l_sc, acc_sc, *, nb, scale):
    blk = MOBA_BLOCK
    h = pl.program_id(1)
    qi = pl.program_id(2)

    @pl.when(qi == 0)
    def _():
        kb_sc[...] = k_ref[...].astype(BF16)
        vb_sc[...] = v_ref[...].astype(BF16)
        km_sc[...] = jnp.zeros_like(km_sc)
        for n in range(nb):
            km_sc[n:n + 1, :] = jnp.sum(k_ref[n * blk:(n + 1) * blk, :], axis=0, keepdims=True) * (1.0 / blk)

    q32 = q_ref[...]
    qb = q32.astype(BF16)
    gate = lax.dot_general(q32, km_sc[...], NT_DIMS, precision=lax.Precision.HIGHEST,
                           preferred_element_type=F32)
    lane = lax.broadcasted_iota(jnp.int32, gate.shape, 1)
    gm = jnp.where(lane < qi, gate, -jnp.inf)
    rank = jnp.zeros(gate.shape, jnp.int32)
    for m in range(nb - 1):
        col = gm[:, m:m + 1]
        ahead = (col > gm) | ((col == gm) & (m < lane))
        rank = rank + ahead.astype(jnp.int32)
    addm = jnp.where((rank < MOBA_TOPK) & (lane < qi), 0.0, NEG)

    m_sc[...] = jnp.full_like(m_sc, NEG)
    l_sc[...] = jnp.zeros_like(l_sc)
    acc_sc[...] = jnp.zeros_like(acc_sc)

    def step(start, bias_add):
        k = kb_sc[pl.ds(start, blk), :]
        v = vb_sc[pl.ds(start, blk), :]
        s = lax.dot_general(qb, k, NT_DIMS, preferred_element_type=F32) * scale + bias_add
        _softmax_step(s, v, m_sc, l_sc, acc_sc)

    step(pl.multiple_of(qi * blk, blk), bias_ref[0:blk, :])
    far_bias = rb_ref[h * NUM_BUCKETS + NUM_BUCKETS - 1]
    for n in range(nb - 1):
        @pl.when(n < qi)
        def _(n=n):
            bias_add = jnp.where(n == qi - 1, bias_ref[blk:2 * blk, :], far_bias)
            step(n * blk, bias_add + addm[:, n:n + 1])

    o_ref[...] = (acc_sc[...] / l_sc[...]).astype(o_ref.dtype)


def _moba_prompt(proj, bias_tab, rb_flat, batch, seq, n_heads):
    blk = MOBA_BLOCK
    assert seq % blk == 0 and blk == ATT_TILE
    nb = seq // blk
    assert nb <= HEAD_DIM
    kern = functools.partial(_moba_prompt_kernel, nb=nb, scale=HEAD_DIM ** -0.5)
    vmem = 2 * (2 * seq * HEAD_DIM * 4) + 2 * seq * HEAD_DIM * 2 + 2 * 2 * blk * blk * 4 + 16 * blk * blk * 4
    return pl.pallas_call(
        kern,
        grid_spec=pltpu.PrefetchScalarGridSpec(
            num_scalar_prefetch=1, grid=(batch, n_heads, nb),
            in_specs=[pl.BlockSpec((blk, HEAD_DIM), lambda b, h, i, rb: (b * nb + i, h)),
                      pl.BlockSpec((seq, HEAD_DIM), lambda b, h, i, rb: (b, n_heads + h)),
                      pl.BlockSpec((seq, HEAD_DIM), lambda b, h, i, rb: (b, 2 * n_heads + h)),
                      pl.BlockSpec((None, 2 * blk, blk), lambda b, h, i, rb: (h, 0, 0))],
            out_specs=pl.BlockSpec((blk, HEAD_DIM), lambda b, h, i, rb: (b * nb + i, h)),
            scratch_shapes=[pltpu.VMEM((seq, HEAD_DIM), BF16), pltpu.VMEM((seq, HEAD_DIM), BF16),
                            pltpu.VMEM((HEAD_DIM, HEAD_DIM), F32),
                            pltpu.VMEM((blk, 1), F32), pltpu.VMEM((blk, 1), F32),
                            pltpu.VMEM((blk, HEAD_DIM), F32)]),
        out_shape=jax.ShapeDtypeStruct((batch * seq, n_heads * HEAD_DIM), BF16),
        compiler_params=_params(("arbitrary", "arbitrary", "arbitrary"), vmem),
    )(rb_flat, proj, proj, proj, bias_tab)


def _diff_prompt_kernel(rb_ref, q_ref, k_ref, v_ref, bias_ref, lq1, lk1, lq2, lk2, g_ref, o_ref,
                        kb_sc, vb_sc, m_sc, l_sc, acc_sc, *, head0, lam_init, scale):
    t = ATT_TILE
    dh = HEAD_DIM
    h = pl.program_id(1)
    qi = pl.program_id(2)

    @pl.when(qi == 0)
    def _():
        kb_sc[...] = k_ref[...].astype(BF16)
        vb_sc[...] = v_ref[...].astype(BF16)

    qb = q_ref[...].astype(BF16)
    m_sc[...] = jnp.full_like(m_sc, NEG)
    l_sc[...] = jnp.zeros_like(l_sc)
    acc_sc[...] = jnp.zeros_like(acc_sc)

    def step(start, bias_add):
        v = vb_sc[pl.ds(start, t), :]
        for c in range(2):
            k = kb_sc[pl.ds(start, t), c * dh:(c + 1) * dh]
            s = lax.dot_general(qb[:, c * dh:(c + 1) * dh], k, NT_DIMS, preferred_element_type=F32) * scale + bias_add
            _softmax_step(s, v, m_sc.at[c], l_sc.at[c], acc_sc.at[c])

    far_bias = rb_ref[(head0 + h) * NUM_BUCKETS + NUM_BUCKETS - 1]

    def far(j, carry):
        step(pl.multiple_of(j * t, t), far_bias)
        return carry

    lax.fori_loop(0, qi - 1, far, 0)

    @pl.when(qi >= 1)
    def _():
        step(pl.multiple_of((qi - 1) * t, t), bias_ref[t:2 * t, :])

    step(pl.multiple_of(qi * t, t), bias_ref[0:t, :])

    lam = _diff_lambda(lq1, lk1, lq2, lk2, lam_init)
    o = acc_sc[0] / l_sc[0] - lam * (acc_sc[1] / l_sc[1])
    o_ref[...] = _sub_rmsnorm(o, g_ref, lam_init).astype(o_ref.dtype)


def _diff_prompt(proj, bias_tab, rb_flat, lams, sub_g, batch, seq, n_heads, col0, head0, lam_init):
    t = ATT_TILE
    w = 2 * HEAD_DIM
    assert seq % t == 0 and col0 % w == 0
    nq = seq // t
    c0 = col0 // w
    kern = functools.partial(_diff_prompt_kernel, head0=head0, lam_init=lam_init, scale=HEAD_DIM ** -0.5)
    vec = pl.BlockSpec((1, HEAD_DIM), lambda b, h, i, rb: (0, 0))
    vmem = 2 * (2 * seq * w * 4) + 2 * seq * w * 2 + 2 * 2 * t * t * 4 + 24 * t * t * 4
    return pl.pallas_call(
        kern,
        grid_spec=pltpu.PrefetchScalarGridSpec(
            num_scalar_prefetch=1, grid=(batch, n_heads, nq),
            in_specs=[pl.BlockSpec((t, w), lambda b, h, i, rb: (b * nq + i, c0 + h)),
                      pl.BlockSpec((seq, w), lambda b, h, i, rb: (b, c0 + n_heads + h)),
                      pl.BlockSpec((seq, w), lambda b, h, i, rb: (b, c0 + 2 * n_heads + h)),
                      pl.BlockSpec((None, 2 * t, t), lambda b, h, i, rb: (head0 + h, 0, 0)),
                      vec, vec, vec, vec,
                      pl.BlockSpec((1, w), lambda b, h, i, rb: (0, 0))],
            out_specs=pl.BlockSpec((t, w), lambda b, h, i, rb: (b * nq + i, h)),
            scratch_shapes=[pltpu.VMEM((seq, w), BF16), pltpu.VMEM((seq, w), BF16),
                            pltpu.VMEM((2, t, 1), F32), pltpu.VMEM((2, t, 1), F32),
                            pltpu.VMEM((2, t, w), F32)]),
        out_shape=jax.ShapeDtypeStruct((batch * seq, n_heads * w), BF16),
        compiler_params=_params(("arbitrary", "arbitrary", "arbitrary"), vmem),
    )(rb_flat, proj, proj, proj, bias_tab, *lams, sub_g.reshape(1, w))


def _sb_weights(z, tri, carry, past):
    log1pe = jnp.log(1.0 + jnp.exp(-jnp.abs(z)))
    log_beta = jnp.minimum(z, 0.0) - log1pe
    log_keep = jnp.minimum(-z, 0.0) - log1pe
    if past is not None:
        log_keep = jnp.where(past, log_keep, 0.0)
    hi = log_keep.astype(BF16)
    r1 = log_keep - hi.astype(F32)
    mid = r1.astype(BF16)
    lo = (r1 - mid.astype(F32)).astype(BF16)
    after = (jnp.dot(hi, tri, preferred_element_type=F32) + jnp.dot(mid, tri, preferred_element_type=F32)
             + jnp.dot(lo, tri, preferred_element_type=F32))
    a = jnp.exp(log_beta + (after + carry))
    if past is not None:
        a = jnp.where(past, a, 0.0)
    return a, carry + (after[:, 0:1] + log_keep[:, 0:1])


def _sb_prompt_kernel(q_ref, k_ref, v_ref, tri_ref, o_ref, kb_sc, vb_sc, *, scale):
    t = SB_TILE
    qi = pl.program_id(2)

    @pl.when(qi == 0)
    def _():
        kb_sc[...] = k_ref[...].astype(BF16)
        vb_sc[...] = v_ref[...].astype(BF16)

    qb = q_ref[...].astype(BF16)
    tri = tri_ref[...]

    def step(start, carry, acc, past):
        k = kb_sc[pl.ds(start, t), :]
        v = vb_sc[pl.ds(start, t), :]
        z = lax.dot_general(qb, k, NT_DIMS, preferred_element_type=F32) * scale
        a, carry = _sb_weights(z, tri, carry, past)
        return carry, acc + jnp.dot(a.astype(BF16), v, preferred_element_type=F32)

    row = lax.broadcasted_iota(jnp.int32, (t, t), 0)
    col = lax.broadcasted_iota(jnp.int32, (t, t), 1)
    carry, acc = step(pl.multiple_of(qi * t, t), jnp.zeros((t, 1), F32), jnp.zeros((t, HEAD_DIM), F32), col < row)

    def body(j, ca):
        return step(pl.multiple_of((qi - j) * t, t), ca[0], ca[1], None)

    carry, acc = lax.fori_loop(1, qi + 1, body, (carry, acc))
    o_ref[...] = acc.astype(o_ref.dtype)


def _tri_strict():
    j = np.arange(SB_TILE)
    return jnp.asarray((j[:, None] > j[None, :]).astype(np.float32), dtype=BF16)


def _sb_prompt(proj, batch, seq, n_heads):
    t = SB_TILE
    assert seq % t == 0
    nq = seq // t
    vmem = 2 * (2 * seq * HEAD_DIM * 4) + 2 * seq * HEAD_DIM * 2 + 24 * t * t * 4
    return pl.pallas_call(
        functools.partial(_sb_prompt_kernel, scale=HEAD_DIM ** -0.5),
        grid=(batch, n_heads, nq),
        in_specs=[pl.BlockSpec((t, HEAD_DIM), lambda b, h, i: (b * nq + i, h)),
                  pl.BlockSpec((seq, HEAD_DIM), lambda b, h, i: (b, n_heads + h)),
                  pl.BlockSpec((seq, HEAD_DIM), lambda b, h, i: (b, 2 * n_heads + h)),
                  pl.BlockSpec((t, t), lambda b, h, i: (0, 0))],
        out_specs=pl.BlockSpec((t, HEAD_DIM), lambda b, h, i: (b * nq + i, h)),
        out_shape=jax.ShapeDtypeStruct((batch * seq, n_heads * HEAD_DIM), BF16),
        scratch_shapes=[pltpu.VMEM((seq, HEAD_DIM), BF16), pltpu.VMEM((seq, HEAD_DIM), BF16)],
        compiler_params=_params(("arbitrary", "arbitrary", "arbitrary"), vmem),
    )(proj, proj, proj, _tri_strict())


def _block_diag_rows(q, width):
    s, r, dh = q.shape
    assert width == r * dh
    eye = jnp.eye(r, dtype=q.dtype)
    return (eye[None, :, :, None] * q[:, :, None, :]).reshape(s, r, width).astype(BF16)


def _new_page(rows):
    s, w = rows.shape
    return jnp.zeros((s, PAGE_SIZE, w), rows.dtype).at[:, 0, :].set(rows)


def _diff_sample_kernel(pt_ref, qbd_ref, kc_ref, vc_ref, kn_ref, vn_ref, brow_ref, lq1, lk1, lq2, lk2, g_ref,
                        o_ref, m_sc, l_sc, acc_sc, *, n_pages, n_heads, lam_init, scale):
    p = pl.program_id(1)
    w = 2 * HEAD_DIM

    @pl.when(p == 0)
    def _():
        m_sc[...] = jnp.full_like(m_sc, NEG)
        l_sc[...] = jnp.zeros_like(l_sc)
        acc_sc[...] = jnp.zeros_like(acc_sc)

    def process(k_ref, v_ref, kind):
        s = lax.dot_general(qbd_ref[...], k_ref[...].astype(BF16), NT_DIMS, preferred_element_type=F32)
        s = s * scale + brow_ref[kind]
        _softmax_step(s, v_ref[...].astype(BF16), m_sc, l_sc, acc_sc)

    @pl.when(p < n_pages)
    def _():
        process(kc_ref, vc_ref, jnp.where(p == n_pages - 1, 1, 0))

    @pl.when(p == n_pages)
    def _():
        process(kn_ref, vn_ref, 2)
        lam = _diff_lambda(lq1, lk1, lq2, lk2, lam_init)
        for h in range(n_heads):
            o0 = acc_sc[2 * h:2 * h + 1, h * w:(h + 1) * w] / l_sc[2 * h:2 * h + 1, :]
            o1 = acc_sc[2 * h + 1:2 * h + 2, h * w:(h + 1) * w] / l_sc[2 * h + 1:2 * h + 2, :]
            o_ref[h:h + 1, :] = _sub_rmsnorm(o0 - lam * o1, g_ref, lam_init)


def _diff_sample(q, k_new, v_new, cache_k, cache_v, page_table, brow, lams, sub_g, lam_init):
    s, r, dh = q.shape
    n_heads = r // 2
    width = r * dh
    n_pages = page_table.shape[1]
    kern = functools.partial(_diff_sample_kernel, n_pages=n_pages, n_heads=n_heads, lam_init=lam_init,
                             scale=HEAD_DIM ** -0.5)
    page = lambda i, p, pt: (pt[i, jnp.minimum(p, n_pages - 1)], 0, 0)
    mine = lambda i, p, pt: (i, 0, 0)
    vec = pl.BlockSpec((1, HEAD_DIM), lambda i, p, pt: (0, 0))
    vmem = 2 * 4 * PAGE_SIZE * width * 4 + 4 * PAGE_SIZE * width * 2
    return pl.pallas_call(
        kern,
        grid_spec=pltpu.PrefetchScalarGridSpec(
            num_scalar_prefetch=1, grid=(s, n_pages + 1),
            in_specs=[pl.BlockSpec((None, r, width), mine),
                      pl.BlockSpec((None, PAGE_SIZE, width), page),
                      pl.BlockSpec((None, PAGE_SIZE, width), page),
                      pl.BlockSpec((None, PAGE_SIZE, width), mine),
                      pl.BlockSpec((None, PAGE_SIZE, width), mine),
                      pl.BlockSpec((3, r, PAGE_SIZE), lambda i, p, pt: (0, 0, 0)),
                      vec, vec, vec, vec,
                      pl.BlockSpec((1, 2 * HEAD_DIM), lambda i, p, pt: (0, 0))],
            out_specs=pl.BlockSpec((None, n_heads, 2 * HEAD_DIM), mine),
            scratch_shapes=[pltpu.VMEM((r, 1), F32), pltpu.VMEM((r, 1), F32), pltpu.VMEM((r, width), F32)]),
        out_shape=jax.ShapeDtypeStruct((s, n_heads, 2 * HEAD_DIM), F32),
        compiler_params=_params(("arbitrary", "arbitrary"), vmem),
    )(page_table, _block_diag_rows(q, width), cache_k, cache_v, _new_page(k_new), _new_page(v_new), brow,
      *lams, sub_g.reshape(1, 2 * HEAD_DIM))


def _sb_sample_kernel(pt_ref, qbd_ref, kc_ref, vc_ref, tri_ref, o_ref, c_sc, acc_sc, *, n_pages, n_heads, scale):
    p = pl.program_id(1)

    @pl.when(p == 0)
    def _():
        c_sc[...] = jnp.zeros_like(c_sc)
        acc_sc[...] = jnp.zeros_like(acc_sc)

    z = lax.dot_general(qbd_ref[...], kc_ref[...].astype(BF16), NT_DIMS, preferred_element_type=F32) * scale
    a, carry = _sb_weights(z, tri_ref[...], c_sc[...], None)
    c_sc[...] = carry
    acc_sc[...] += jnp.dot(a.astype(BF16), vc_ref[...].astype(BF16), preferred_element_type=F32)

    @pl.when(p == n_pages - 1)
    def _():
        row = lax.broadcasted_iota(jnp.int32, (n_heads, HEAD_DIM), 0)
        out = jnp.zeros((n_heads, HEAD_DIM), F32)
        for h in range(n_heads):
            out = out + jnp.where(row == h, acc_sc[:, h * HEAD_DIM:(h + 1) * HEAD_DIM], 0.0)
        o_ref[...] = out


def _sb_sample(q, cache_k, cache_v, page_table):
    s, n_heads, dh = q.shape
    width = n_heads * dh
    n_pages = page_table.shape[1]
    kern = functools.partial(_sb_sample_kernel, n_pages=n_pages, n_heads=n_heads, scale=HEAD_DIM ** -0.5)
    page = lambda i, p, pt: (pt[i, n_pages - 1 - p], 0, 0)
    mine = lambda i, p, pt: (i, 0, 0)
    vmem = 2 * 2 * PAGE_SIZE * width * 4 + 4 * PAGE_SIZE * width * 2
    return pl.pallas_call(
        kern,
        grid_spec=pltpu.PrefetchScalarGridSpec(
            num_scalar_prefetch=1, grid=(s, n_pages),
            in_specs=[pl.BlockSpec((None, n_heads, width), mine),
                      pl.BlockSpec((None, PAGE_SIZE, width), page),
                      pl.BlockSpec((None, PAGE_SIZE, width), page),
                      pl.BlockSpec((SB_TILE, SB_TILE), lambda i, p, pt: (0, 0))],
            out_specs=pl.BlockSpec((None, n_heads, dh), mine),
            scratch_shapes=[pltpu.VMEM((n_heads, 1), F32), pltpu.VMEM((n_heads, width), F32)]),
        out_shape=jax.ShapeDtypeStruct((s, n_heads, dh), F32),
        compiler_params=_params(("arbitrary", "arbitrary"), vmem),
    )(page_table, _block_diag_rows(q, width), cache_k, cache_v, _tri_strict())


def _moba_kmean_kernel(pt_ref, ka_ref, kb_ref, o_ref):
    tot = jnp.sum(ka_ref[...], axis=0, keepdims=True) + jnp.sum(kb_ref[...], axis=0, keepdims=True)
    o_ref[...] = tot * (1.0 / MOBA_BLOCK)


def _moba_kmean(cache_k, page_table):
    assert MOBA_BLOCK == 2 * PAGE_SIZE
    s, n_pages = page_table.shape
    assert n_pages % 2 == 0
    nblk = n_pages // 2
    width = cache_k.shape[2]
    out = pl.pallas_call(
        _moba_kmean_kernel,
        grid_spec=pltpu.PrefetchScalarGridSpec(
            num_scalar_prefetch=1, grid=(s, nblk),
            in_specs=[pl.BlockSpec((None, PAGE_SIZE, width), lambda i, n, pt: (pt[i, 2 * n], 0, 0)),
                      pl.BlockSpec((None, PAGE_SIZE, width), lambda i, n, pt: (pt[i, 2 * n + 1], 0, 0))],
            out_specs=pl.BlockSpec((None, None, 1, width), lambda i, n, pt: (i, n, 0, 0))),
        out_shape=jax.ShapeDtypeStruct((s, nblk, 1, width), F32),
        compiler_params=_params(("arbitrary", "arbitrary"), 6 * PAGE_SIZE * width * 4),
    )(page_table, cache_k, cache_k)
    return out.reshape(s, nblk, width)


def _moba_top_kernel(km_ref, q_ref, o_ref, *, n_heads):
    nblk = km_ref.shape[0]
    prod = km_ref[...] * q_ref[...]
    lane = lax.broadcasted_iota(jnp.int32, (nblk, HEAD_DIM), 1)
    rows = lax.broadcasted_iota(jnp.int32, (nblk, HEAD_DIM), 0).astype(F32)
    gates = jnp.zeros((nblk, HEAD_DIM), F32)
    for h in range(n_heads):
        col = jnp.sum(prod[:, h * HEAD_DIM:(h + 1) * HEAD_DIM], axis=1, keepdims=True)
        gates = jnp.where(lane == h, col, gates)
    out_row = lax.broadcasted_iota(jnp.int32, (8, HEAD_DIM), 0)
    out = jnp.zeros((8, HEAD_DIM), F32)
    for r in range(MOBA_TOPK):
        best = jnp.max(gates, axis=0, keepdims=True)
        idx = jnp.min(jnp.where(gates == best, rows, float(nblk)), axis=0, keepdims=True)
        out = jnp.where(out_row == r, idx, out)
        gates = jnp.where(rows == idx, -jnp.inf, gates)
    o_ref[...] = out.astype(jnp.int32)


def _moba_top(kmean, q_flat, n_heads):
    s, nblk, width = kmean.shape
    assert nblk >= MOBA_TOPK and n_heads <= HEAD_DIM
    return pl.pallas_call(
        functools.partial(_moba_top_kernel, n_heads=n_heads),
        grid=(s,),
        in_specs=[pl.BlockSpec((None, nblk, width), lambda i: (i, 0, 0)),
                  pl.BlockSpec((None, 1, width), lambda i: (i, 0, 0))],
        out_specs=pl.BlockSpec((None, 8, HEAD_DIM), lambda i: (i, 0, 0)),
        out_shape=jax.ShapeDtypeStruct((s, 8, HEAD_DIM), jnp.int32),
        compiler_params=_params(("arbitrary",), 6 * nblk * width * 4),
    )(kmean, q_flat)


def _moba_attend_kernel(phys_ref, logi_ref, q_ref, kc_ref, vc_ref, kn_ref, vn_ref, brow_ref, o_ref,
                        m_sc, l_sc, acc_sc, *, n_sel, n_pages, scale):
    i = pl.program_id(0)
    h = pl.program_id(1)
    j = pl.program_id(2)

    @pl.when(j == 0)
    def _():
        m_sc[...] = jnp.full_like(m_sc, NEG)
        l_sc[...] = jnp.zeros_like(l_sc)
        acc_sc[...] = jnp.zeros_like(acc_sc)

    def process(k_ref, v_ref, kind):
        s = lax.dot_general(q_ref[...], k_ref[...].astype(BF16), NT_DIMS, preferred_element_type=F32)
        s = s * scale + brow_ref[pl.ds(kind, 1), :]
        _softmax_step(s, v_ref[...].astype(BF16), m_sc, l_sc, acc_sc)

    @pl.when(j < n_sel)
    def _():
        logical = logi_ref[i, h, j]
        process(kc_ref, vc_ref, jnp.where(logical == n_pages - 1, 1, 0))

    @pl.when(j == n_sel)
    def _():
        process(kn_ref, vn_ref, 2)
        o_ref[...] = acc_sc[...] / l_sc[...]


def _moba_attend(q, k_new, v_new, cache_k, cache_v, phys, logical, brow, n_pages):
    s, n_heads, dh = q.shape
    n_sel = phys.shape[2]
    q8 = jnp.zeros((s, n_heads, 8, dh), BF16).at[:, :, 0, :].set(q.astype(BF16))
    kern = functools.partial(_moba_attend_kernel, n_sel=n_sel, n_pages=n_pages, scale=HEAD_DIM ** -0.5)
    page = lambda i, h, j, ph, lg: (ph[i, h, jnp.minimum(j, n_sel - 1)], 0, h)
    mine = lambda i, h, j, ph, lg: (i, 0, h)
    out = pl.pallas_call(
        kern,
        grid_spec=pltpu.PrefetchScalarGridSpec(
            num_scalar_prefetch=2, grid=(s, n_heads, n_sel + 1),
            in_specs=[pl.BlockSpec((None, None, 8, dh), lambda i, h, j, ph, lg: (i, h, 0, 0)),
                      pl.BlockSpec((None, PAGE_SIZE, dh), page),
                      pl.BlockSpec((None, PAGE_SIZE, dh), page),
                      pl.BlockSpec((None, PAGE_SIZE, dh), mine),
                      pl.BlockSpec((None, PAGE_SIZE, dh), mine),
                      pl.BlockSpec((None, 8, PAGE_SIZE), lambda i, h, j, ph, lg: (h, 0, 0))],
            out_specs=pl.BlockSpec((None, None, 8, dh), lambda i, h, j, ph, lg: (i, h, 0, 0)),
            scratch_shapes=[pltpu.VMEM((8, 1), F32), pltpu.VMEM((8, 1), F32), pltpu.VMEM((8, dh), F32)]),
        out_shape=jax.ShapeDtypeStruct((s, n_heads, 8, dh), F32),
        compiler_params=_params(("arbitrary", "arbitrary", "arbitrary"), 16 * PAGE_SIZE * dh * 4),
    )(phys, logical, q8, cache_k, cache_v, _new_page(k_new), _new_page(v_new), brow)
    return out[:, :, 0, :]


def _moba_sample(q, k_new, v_new, cache_k, cache_v, page_table, brow):
    s, n_heads, dh = q.shape
    n_pages = page_table.shape[1]
    kmean = _moba_kmean(cache_k, page_table)
    top = _moba_top(kmean, q.reshape(s, 1, n_heads * dh), n_heads)
    blocks = top[:, :MOBA_TOPK, :n_heads].transpose(0, 2, 1)
    logical = (2 * blocks[..., None] + jnp.arange(2, dtype=jnp.int32)).reshape(s, n_heads, 2 * MOBA_TOPK)
    phys = jnp.take_along_axis(page_table[:, None, :], logical, axis=2)
    return _moba_attend(q, k_new, v_new, cache_k, cache_v, phys, logical, brow, n_pages)


MM_PROMPT = dict(tm=1024, tn=512, tk=4096)
MM_PROMPT_DOWN = dict(tm=512, tn=512, tk=5504)
MM_SAMPLE = dict(tm=16, tn=512, tk=16384)


def _lam_init(layer):
    return 0.8 - 0.6 * math.exp(-0.3 * layer)


def _ffn_and_ple(h, p_rows, gate_fn, i, wts, mm):
    f = _rmsnorm(h, wts['g_ffn'][i], BF16)
    gu = _mm(f, wts['w_up'][i], **mm)
    act = gate_fn(gu)
    down = dict(mm, **MM_PROMPT_DOWN) if mm is MM_PROMPT else mm
    h = _mm(act, wts['w_down'][i], extras=(h,), epilogue=_add_residual, **down)
    pp = _mm(p_rows[i], wts['w_ple_proj'][i], **mm)
    a = _rmsnorm(h, wts['g_ple'][i], BF16)
    h = _mm(a, wts['w_ple_gate'][i], extras=(h, pp), epilogue=_ple_combine, **mm)
    return h, gu


def kernel(x_prompt, x_sample, cache_moba_k, cache_moba_v, cache_diff_k, cache_diff_v, cache_sb_k, cache_sb_v,
           state_conv, page_table, p_prompt, p_sample, rel_bias, w_in_even, w_out_even, lam_q1, lam_k1, lam_q2,
           lam_k2, diff_subln, w_in_odd, w_out_odd, g_mix, g_ffn, w_up, conv_w, conv_b, w_down, g_ple,
           w_ple_gate, w_ple_proj, g_final):
    batch, seq, d = x_prompt.shape
    dec = x_sample.shape[0]
    depth = g_mix.shape[0]
    assert depth == 2 and x_sample.shape[1] == 1
    dh = HEAD_DIM
    moba_heads = cache_moba_k.shape[3]
    diff_heads = cache_diff_k.shape[3]
    sb_heads = cache_sb_k.shape[3]
    moba_w, diff_w, sb_w = moba_heads * dh, diff_heads * 2 * dh, sb_heads * dh
    dff = conv_w.shape[2]
    n_pages = page_table.shape[1]
    n_pool = cache_moba_k.shape[1]
    m_p = batch * seq
    m_s = 16

    per_layer = lambda w: [w[i].astype(BF16) for i in range(depth)]
    wts = dict(g_ffn=g_ffn, g_ple=g_ple, w_up=per_layer(w_up), w_down=per_layer(w_down),
               w_ple_gate=per_layer(w_ple_gate), w_ple_proj=per_layer(w_ple_proj))
    w_in_e, w_out_e = w_in_even[0].astype(BF16), w_out_even[0].astype(BF16)
    w_in_o, w_out_o = w_in_odd[0].astype(BF16), w_out_odd[0].astype(BF16)
    lams = tuple(v[0].reshape(1, dh) for v in (lam_q1, lam_k1, lam_q2, lam_k2))
    sub_g = diff_subln[0]
    lam0 = _lam_init(0)

    rb_flat = rel_bias.T.reshape(-1)
    bias_tab = _prompt_bias_tables(rb_flat)
    bias_rows = _sample_bias_rows(rb_flat)

    h = x_prompt.reshape(m_p, d)
    pp_rows = p_prompt.reshape(depth, m_p, -1).astype(BF16)

    a = _rmsnorm(h, g_mix[0], BF16)
    proj_e = _mm(a, w_in_e, **MM_PROMPT)
    mo = _moba_prompt(proj_e, bias_tab, rb_flat, batch, seq, moba_heads)
    do = _diff_prompt(proj_e, bias_tab, rb_flat, lams, sub_g, batch, seq, diff_heads,
                      col0=3 * moba_w, head0=moba_heads, lam_init=lam0)
    h = _mm(mo, w_out_e[:moba_w], extras=(h,), epilogue=_add_residual, **MM_PROMPT)
    h = _mm(do, w_out_e[moba_w:], extras=(h,), epilogue=_add_residual, **MM_PROMPT)
    gate_p = lambda i: (lambda gu: _ffn_gate_prompt(gu, conv_w[i], conv_b[i], seq))
    h, gu0 = _ffn_and_ple(h, pp_rows, gate_p(0), 0, wts, MM_PROMPT)

    a = _rmsnorm(h, g_mix[1], BF16)
    proj_o = _mm(a, w_in_o, **MM_PROMPT)
    so = _sb_prompt(proj_o, batch, seq, sb_heads)
    h = _mm(so, w_out_o, extras=(h,), epilogue=_add_residual, **MM_PROMPT)
    h, gu1 = _ffn_and_ple(h, pp_rows, gate_p(1), 1, wts, MM_PROMPT)
    y_prompt = _rmsnorm(h, g_final, F32).reshape(batch, seq, d)

    def prompt_rows(proj, lo, width, shape):
        return proj[:, lo:lo + width].reshape((1, batch, seq) + shape)

    moba_k_p = prompt_rows(proj_e, moba_w, moba_w, (moba_heads, dh))
    moba_v_p = prompt_rows(proj_e, 2 * moba_w, moba_w, (moba_heads, dh))
    diff_k_p = prompt_rows(proj_e, 3 * moba_w + diff_w, diff_w, (diff_heads, 2, dh))
    diff_v_p = prompt_rows(proj_e, 3 * moba_w + 2 * diff_w, diff_w, (diff_heads, 2 * dh))
    sb_k_p = prompt_rows(proj_o, sb_w, sb_w, (sb_heads, dh))
    sb_v_p = prompt_rows(proj_o, 2 * sb_w, sb_w, (sb_heads, dh))
    conv_p = jnp.stack([g[:, :dff].reshape(batch, seq, dff)[:, seq - (CONV_W - 1):] for g in (gu0, gu1)])

    pad = lambda r: jnp.zeros((m_s,) + r.shape[1:], r.dtype).at[:dec].set(r)
    hs = pad(x_sample.reshape(dec, d))
    ps_rows = jnp.stack([pad(p_sample[i].reshape(dec, -1)) for i in range(depth)]).astype(BF16)

    a = _rmsnorm(hs, g_mix[0], BF16)
    pe = _mm(a, w_in_e, **MM_SAMPLE)[:dec]
    mq, mk, mv = pe[:, :moba_w], pe[:, moba_w:2 * moba_w], pe[:, 2 * moba_w:3 * moba_w]
    o3 = 3 * moba_w
    dq, dk, dv = pe[:, o3:o3 + diff_w], pe[:, o3 + diff_w:o3 + 2 * diff_w], pe[:, o3 + 2 * diff_w:]
    brow_m = bias_rows[:moba_heads]
    brow_d = jnp.repeat(bias_rows[moba_heads:, :3], 2, axis=0).transpose(1, 0, 2)
    mo_s = _moba_sample(mq.reshape(dec, moba_heads, dh), mk, mv,
                        cache_moba_k[0].reshape(n_pool, PAGE_SIZE, moba_w),
                        cache_moba_v[0].reshape(n_pool, PAGE_SIZE, moba_w), page_table, brow_m)
    do_s = _diff_sample(dq.reshape(dec, 2 * diff_heads, dh), dk, dv,
                        cache_diff_k[0].reshape(n_pool, PAGE_SIZE, diff_w),
                        cache_diff_v[0].reshape(n_pool, PAGE_SIZE, diff_w), page_table, brow_d,
                        lams, sub_g, lam0)
    att = pad(jnp.concatenate([mo_s.reshape(dec, moba_w), do_s.reshape(dec, diff_w)], axis=1)).astype(BF16)
    hs = _mm(att, w_out_e, extras=(hs,), epilogue=_add_residual, **MM_SAMPLE)

    def gate_s(i):
        s0, s1 = pad(state_conv[i, :, 0]), pad(state_conv[i, :, 1])
        return lambda gu: _ffn_gate_sample(gu, s0, s1, conv_w[i], conv_b[i])

    hs, gs0 = _ffn_and_ple(hs, ps_rows, gate_s(0), 0, wts, MM_SAMPLE)

    a = _rmsnorm(hs, g_mix[1], BF16)
    po = _mm(a, w_in_o, **MM_SAMPLE)[:dec]
    sq, sk, sv = po[:, :sb_w], po[:, sb_w:2 * sb_w], po[:, 2 * sb_w:]
    so_s = _sb_sample(sq.reshape(dec, sb_heads, dh), cache_sb_k[0].reshape(n_pool, PAGE_SIZE, sb_w),
                      cache_sb_v[0].reshape(n_pool, PAGE_SIZE, sb_w), page_table)
    hs = _mm(pad(so_s.reshape(dec, sb_w)).astype(BF16), w_out_o, extras=(hs,), epilogue=_add_residual, **MM_SAMPLE)
    hs, gs1 = _ffn_and_ple(hs, ps_rows, gate_s(1), 1, wts, MM_SAMPLE)
    y_sample = _rmsnorm(hs, g_final, F32)[:dec].reshape(dec, 1, d)

    conv_s = jnp.stack([jnp.stack([state_conv[i, :, 1], g[:dec, :dff]], axis=1) for i, g in enumerate((gs0, gs1))])

    return (y_prompt, y_sample,
            moba_k_p, moba_v_p, diff_k_p, diff_v_p, sb_k_p, sb_v_p, conv_p,
            mk.reshape(1, dec, 1, moba_heads, dh), mv.reshape(1, dec, 1, moba_heads, dh),
            dk.reshape(1, dec, 1, diff_heads, 2, dh), dv.reshape(1, dec, 1, diff_heads, 2 * dh),
            sk.reshape(1, dec, 1, sb_heads, dh), sv.reshape(1, dec, 1, sb_heads, dh), conv_s)
```

```python
import functools
import math

import numpy as np
import jax
import jax.numpy as jnp
from jax import lax
from jax.experimental import pallas as pl
from jax.experimental.pallas import tpu as pltpu

F32 = jnp.float32
BF16 = jnp.bfloat16

HEAD_DIM = 128
MOBA_BLOCK = 256
MOBA_TOPK = 3
NUM_BUCKETS = 32
MAX_DISTANCE = 128
PAGE_SIZE = 128
CONV_W = 3
EPS = 1e-6
NEG = -1e30
ATT_TILE = 256
SB_TQ = 512
SB_TK = 256
LANES = 128
BF16_SUBLANES = 16
V7X_VMEM_BYTES = 64 * 1024 * 1024
VMEM_CAP = V7X_VMEM_BYTES - 8 * 1024 * 1024
NT_DIMS = (((1,), (1,)), ((), ()))


def _params(sem, vmem_bytes):
    limit = int(min(max(vmem_bytes * 5 // 4, 16 * 1024 * 1024), VMEM_CAP))
    return pltpu.CompilerParams(dimension_semantics=sem, vmem_limit_bytes=limit)


def _bucket_np(dist):
    n = np.maximum(dist, 0)
    max_exact = NUM_BUCKETS // 2
    nf = np.maximum(n, 1).astype(np.float32)
    large = max_exact + (np.log(nf / np.float32(max_exact)) / np.float32(math.log(MAX_DISTANCE / max_exact))
                         * np.float32(NUM_BUCKETS - max_exact)).astype(np.int32)
    return np.where(n < max_exact, n, np.minimum(large, NUM_BUCKETS - 1)).astype(np.int32)


def _bias_table_kernel(rb_ref, bkt_ref, mask_ref, o_ref):
    h = pl.program_id(0)
    bkt = bkt_ref[...]
    acc = jnp.zeros(bkt.shape, F32)
    for b in range(NUM_BUCKETS):
        acc = jnp.where(bkt == b, rb_ref[h * NUM_BUCKETS + b], acc)
    o_ref[...] = acc + mask_ref[...]


def _bias_table(rb_flat, bkt, mask):
    n_heads = rb_flat.shape[0] // NUM_BUCKETS
    r, c = bkt.shape
    return pl.pallas_call(
        _bias_table_kernel,
        grid_spec=pltpu.PrefetchScalarGridSpec(
            num_scalar_prefetch=1, grid=(n_heads,),
            in_specs=[pl.BlockSpec((r, c), lambda h, rb: (0, 0)),
                      pl.BlockSpec((r, c), lambda h, rb: (0, 0))],
            out_specs=pl.BlockSpec((None, r, c), lambda h, rb: (h, 0, 0))),
        out_shape=jax.ShapeDtypeStruct((n_heads, r, c), F32),
        compiler_params=_params(("arbitrary",), 8 * r * c * 4),
        name="bias_table",
    )(rb_flat, jnp.asarray(bkt), jnp.asarray(mask))


def _prompt_bias_tables(rb_flat):
    t = ATT_TILE
    r = np.arange(t)[:, None]
    c = np.arange(t)[None, :]
    d0 = r - c
    d1 = t + r - c
    assert int(_bucket_np(np.array([2 * t - (t - 1)]))[0]) == NUM_BUCKETS - 1
    bkt = np.concatenate([_bucket_np(d0), _bucket_np(d1)], axis=0)
    mask = np.concatenate([np.where(d0 >= 0, 0.0, NEG), np.zeros((t, t))], axis=0).astype(np.float32)
    return _bias_table(rb_flat, bkt, mask)


def _sample_bias_rows(rb_flat):
    assert int(_bucket_np(np.array([PAGE_SIZE + 1]))[0]) == NUM_BUCKETS - 1
    j = np.arange(PAGE_SIZE)
    bkt = np.zeros((8, PAGE_SIZE), np.int32)
    mask = np.zeros((8, PAGE_SIZE), np.float32)
    bkt[0] = NUM_BUCKETS - 1
    bkt[1] = _bucket_np(PAGE_SIZE - j)
    bkt[2] = 0
    mask[2, 1:] = NEG
    return _bias_table(rb_flat, bkt, mask)


def _rmsnorm_kernel(x_ref, g_ref, o_ref):
    x = x_ref[...]
    y = x * lax.rsqrt(jnp.mean(x * x, axis=-1, keepdims=True) + EPS)
    o_ref[...] = (y * g_ref[...]).astype(o_ref.dtype)


def _rmsnorm(x, g, out_dtype):
    m, d = x.shape
    tr = min(m, 256)
    assert m % tr == 0
    return pl.pallas_call(
        _rmsnorm_kernel,
        grid=(m // tr,),
        in_specs=[pl.BlockSpec((tr, d), lambda i: (i, 0)),
                  pl.BlockSpec((1, d), lambda i: (0, 0))],
        out_specs=pl.BlockSpec((tr, d), lambda i: (i, 0)),
        out_shape=jax.ShapeDtypeStruct((m, d), out_dtype),
        compiler_params=_params(("parallel",), 4 * tr * d * 4),
        name="rmsnorm",
    )(x, g.reshape(1, d))


def _mm_kernel(*refs, nk, n_extra, epilogue):
    a_ref, w_ref = refs[0], refs[1]
    extra = refs[2:2 + n_extra]
    o_ref = refs[2 + n_extra]

    def finish(acc):
        o_ref[...] = epilogue(acc, *[e[...] for e in extra]).astype(o_ref.dtype)

    if nk == 1:
        finish(jnp.dot(a_ref[...], w_ref[...], preferred_element_type=F32))
    else:
        acc_ref = refs[3 + n_extra]
        k = pl.program_id(2)

        @pl.when(k == 0)
        def _():
            acc_ref[...] = jnp.zeros_like(acc_ref)

        acc_ref[...] += jnp.dot(a_ref[...], w_ref[...], preferred_element_type=F32)

        @pl.when(k == nk - 1)
        def _():
            finish(acc_ref[...])


def _mm(a, w, *, tm, tn, tk, extras=(), epilogue=None, out_dtype=F32, name="matmul"):
    m, kdim = a.shape
    n = w.shape[1]
    tm, tn, tk = min(tm, m), min(tn, n), min(tk, kdim)
    assert m % tm == 0 and n % tn == 0 and kdim % tk == 0
    nk = kdim // tk
    if epilogue is None:
        epilogue = lambda acc: acc
    in_specs = [pl.BlockSpec((tm, tk), lambda i, j, k: (i, k)),
                pl.BlockSpec((tk, tn), lambda i, j, k: (k, j))]
    in_specs += [pl.BlockSpec((tm, tn), lambda i, j, k: (i, j)) for _ in extras]
    scratch = [pltpu.VMEM((tm, tn), F32)] if nk > 1 else []
    vmem = 2 * (tm * tk * 2 + tk * tn * 2) + (2 * len(extras) + 4) * tm * tn * 4
    return pl.pallas_call(
        functools.partial(_mm_kernel, nk=nk, n_extra=len(extras), epilogue=epilogue),
        grid=(m // tm, n // tn, nk),
        in_specs=in_specs,
        out_specs=pl.BlockSpec((tm, tn), lambda i, j, k: (i, j)),
        out_shape=jax.ShapeDtypeStruct((m, n), out_dtype),
        scratch_shapes=scratch,
        compiler_params=_params(("parallel", "parallel", "arbitrary"), vmem),
        name=name,
    )(a, w, *extras)


def _add_residual(acc, h):
    return h + acc


def _ple_combine(acc, h, pp):
    return h + jax.nn.sigmoid(acc) * pp


def _conv_taps(cw_ref, cb_ref, g2, g1, g0):
    gc = cb_ref[...] + cw_ref[0:1, :] * g2
    gc = gc + cw_ref[1:2, :] * g1
    return gc + cw_ref[2:3, :] * g0


def _up_gate_prompt_kernel(a_ref, ap_ref, wg_ref, wu_ref, cw_ref, cb_ref, act_ref, tail_ref, *, seq):
    tm = a_ref.shape[0]
    a = a_ref[...]
    g = jnp.dot(a, wg_ref[...], preferred_element_type=F32)
    u = jnp.dot(a, wu_ref[...], preferred_element_type=F32)
    prev = jnp.dot(ap_ref[...], wg_ref[...], preferred_element_type=F32)
    last = BF16_SUBLANES - 1
    row = lax.broadcasted_iota(jnp.int32, (tm, 1), 0)
    pos = (pl.program_id(0) * tm + row) % seq
    g1 = jnp.where(row == 0, prev[last:last + 1, :], pltpu.roll(g, 1, axis=0))
    g1 = jnp.where(pos >= 1, g1, 0.0)
    g2 = jnp.where(row == 0, prev[last - 1:last, :],
                   jnp.where(row == 1, prev[last:last + 1, :], pltpu.roll(g, 2, axis=0)))
    g2 = jnp.where(pos >= 2, g2, 0.0)
    gc = _conv_taps(cw_ref, cb_ref, g2, g1, g)
    act_ref[...] = (jax.nn.silu(gc) * u).astype(act_ref.dtype)
    tail_ref[...] = g[tm - 8:tm, :]


def _up_gate_prompt(a, w_up, conv_w, conv_b, seq, tm=1024, tn=256):
    m, d = a.shape
    dff = conv_w.shape[1]
    assert m % tm == 0 and dff % tn == 0 and seq % tm == 0
    nj = dff // tn
    per16 = tm // BF16_SUBLANES
    vmem = 2 * (tm * d * 2 + BF16_SUBLANES * d * 2 + 2 * d * tn * 2) + 12 * tm * tn * 4
    return pl.pallas_call(
        functools.partial(_up_gate_prompt_kernel, seq=seq),
        grid=(m // tm, nj),
        in_specs=[pl.BlockSpec((tm, d), lambda i, j: (i, 0)),
                  pl.BlockSpec((BF16_SUBLANES, d), lambda i, j: (jnp.maximum(i * per16 - 1, 0), 0)),
                  pl.BlockSpec((d, tn), lambda i, j: (0, j)),
                  pl.BlockSpec((d, tn), lambda i, j: (0, j + nj)),
                  pl.BlockSpec((CONV_W, tn), lambda i, j: (0, j)),
                  pl.BlockSpec((1, tn), lambda i, j: (0, j))],
        out_specs=[pl.BlockSpec((tm, tn), lambda i, j: (i, j)),
                   pl.BlockSpec((8, tn), lambda i, j: (i, j))],
        out_shape=[jax.ShapeDtypeStruct((m, dff), BF16),
                   jax.ShapeDtypeStruct((m // tm * 8, dff), F32)],
        compiler_params=_params(("parallel", "parallel"), vmem),
        name="up_gate_prompt",
    )(a, a, w_up, w_up, conv_w, conv_b.reshape(1, dff))


def _up_gate_sample_kernel(a_ref, wg_ref, wu_ref, s0_ref, s1_ref, cw_ref, cb_ref, act_ref, g_ref):
    a = a_ref[...]
    g = jnp.dot(a, wg_ref[...], preferred_element_type=F32)
    u = jnp.dot(a, wu_ref[...], preferred_element_type=F32)
    gc = _conv_taps(cw_ref, cb_ref, s0_ref[...], s1_ref[...], g)
    act_ref[...] = (jax.nn.silu(gc) * u).astype(act_ref.dtype)
    g_ref[...] = g


def _up_gate_sample(a, w_up, s0, s1, conv_w, conv_b, tn=256):
    m, d = a.shape
    dff = conv_w.shape[1]
    assert dff % tn == 0
    nj = dff // tn
    col = lambda j: (0, j)
    return pl.pallas_call(
        _up_gate_sample_kernel,
        grid=(nj,),
        in_specs=[pl.BlockSpec((m, d), lambda j: (0, 0)),
                  pl.BlockSpec((d, tn), col),
                  pl.BlockSpec((d, tn), lambda j: (0, j + nj)),
                  pl.BlockSpec((m, tn), col), pl.BlockSpec((m, tn), col),
                  pl.BlockSpec((CONV_W, tn), col), pl.BlockSpec((1, tn), col)],
        out_specs=[pl.BlockSpec((m, tn), col), pl.BlockSpec((m, tn), col)],
        out_shape=[jax.ShapeDtypeStruct((m, dff), BF16), jax.ShapeDtypeStruct((m, dff), F32)],
        compiler_params=_params(("parallel",), 2 * (m * d * 2 + 2 * d * tn * 2) + 16 * m * tn * 4),
        name="up_gate_sample",
    )(a, w_up, w_up, s0, s1, conv_w, conv_b.reshape(1, dff))


def _softmax_step(s, v, m_ref, l_ref, acc_ref):
    m_prev = m_ref[...]
    m_new = jnp.maximum(m_prev, jnp.max(s, axis=1, keepdims=True))
    alpha = jnp.exp(m_prev - m_new)
    p = jnp.exp(s - m_new)
    l_ref[...] = alpha * l_ref[...] + jnp.sum(p, axis=1, keepdims=True)
    acc_ref[...] = alpha * acc_ref[...] + jnp.dot(p.astype(BF16), v, preferred_element_type=F32)
    m_ref[...] = m_new


def _softmax_init(m_ref, l_ref, acc_ref):
    m_ref[...] = jnp.full_like(m_ref, NEG)
    l_ref[...] = jnp.zeros_like(l_ref)
    acc_ref[...] = jnp.zeros_like(acc_ref)


def _diff_lambda(lq1, lk1, lq2, lk2, lam_init):
    return (jnp.exp(jnp.sum(lq1[...] * lk1[...], axis=1, keepdims=True))
            - jnp.exp(jnp.sum(lq2[...] * lk2[...], axis=1, keepdims=True)) + lam_init)


def _sub_rmsnorm(o, g_ref, lam_init):
    return o * lax.rsqrt(jnp.mean(o * o, axis=-1, keepdims=True) + EPS) * (g_ref[...] * (1.0 - lam_init))


def _moba_prompt_kernel(rb_ref, q_ref, k_ref, v_ref, bias_ref, o_ref,
                        kb_sc, vb_sc, km_sc, m_sc, l_sc, acc_sc, *, nb, scale):
    blk = MOBA_BLOCK
    h = pl.program_id(1)
    qi = pl.program_id(2)

    @pl.when(qi == 0)
    def _():
        kb_sc[...] = k_ref[...].astype(BF16)
        vb_sc[...] = v_ref[...].astype(BF16)
        km_sc[...] = jnp.zeros_like(km_sc)
        for n in range(nb):
            km_sc[n:n + 1, :] = jnp.sum(k_ref[n * blk:(n + 1) * blk, :], axis=0, keepdims=True) * (1.0 / blk)

    q32 = q_ref[...]
    qb = q32.astype(BF16)
    gate = lax.dot_general(q32, km_sc[...], NT_DIMS, precision=lax.Precision.HIGHEST,
                           preferred_element_type=F32)
    lane = lax.broadcasted_iota(jnp.int32, gate.shape, 1)
    gm = jnp.where(lane < qi, gate, -jnp.inf)
    rank = jnp.zeros(gate.shape, jnp.int32)
    for m in range(nb - 1):
        col = gm[:, m:m + 1]
        ahead = (col > gm) | ((col == gm) & (m < lane))
        rank = rank + ahead.astype(jnp.int32)
    addm = jnp.where((rank < MOBA_TOPK) & (lane < qi), 0.0, NEG)

    _softmax_init(m_sc, l_sc, acc_sc)

    def step(start, bias_add):
        k = kb_sc[pl.ds(start, blk), :]
        v = vb_sc[pl.ds(start, blk), :]
        s = lax.dot_general(qb, k, NT_DIMS, preferred_element_type=F32) * scale + bias_add
        _softmax_step(s, v, m_sc, l_sc, acc_sc)

    step(pl.multiple_of(qi * blk, blk), bias_ref[0:blk, :])
    far_bias = rb_ref[h * NUM_BUCKETS + NUM_BUCKETS - 1]
    for n in range(nb - 1):
        @pl.when(n < qi)
        def _(n=n):
            bias_add = jnp.where(n == qi - 1, bias_ref[blk:2 * blk, :], far_bias)
            step(n * blk, bias_add + addm[:, n:n + 1])

    o_ref[...] = (acc_sc[...] / l_sc[...]).astype(o_ref.dtype)


def _moba_prompt(q, k, v, bias_tab, rb_flat, batch, seq):
    blk = MOBA_BLOCK
    n_heads = q.shape[1] // HEAD_DIM
    assert seq % blk == 0 and blk == ATT_TILE
    nb = seq // blk
    assert nb <= HEAD_DIM
    kern = functools.partial(_moba_prompt_kernel, nb=nb, scale=HEAD_DIM ** -0.5)
    vmem = 2 * (2 * seq * HEAD_DIM * 4) + 2 * seq * HEAD_DIM * 2 + 2 * 2 * blk * blk * 4 + 16 * blk * blk * 4
    tile = lambda b, h, i, rb: (b * nb + i, h)
    whole = lambda b, h, i, rb: (b, h)
    return pl.pallas_call(
        kern,
        grid_spec=pltpu.PrefetchScalarGridSpec(
            num_scalar_prefetch=1, grid=(batch, n_heads, nb),
            in_specs=[pl.BlockSpec((blk, HEAD_DIM), tile),
                      pl.BlockSpec((seq, HEAD_DIM), whole),
                      pl.BlockSpec((seq, HEAD_DIM), whole),
                      pl.BlockSpec((None, 2 * blk, blk), lambda b, h, i, rb: (h, 0, 0))],
            out_specs=pl.BlockSpec((blk, HEAD_DIM), tile),
            scratch_shapes=[pltpu.VMEM((seq, HEAD_DIM), BF16), pltpu.VMEM((seq, HEAD_DIM), BF16),
                            pltpu.VMEM((HEAD_DIM, HEAD_DIM), F32),
                            pltpu.VMEM((blk, 1), F32), pltpu.VMEM((blk, 1), F32),
                            pltpu.VMEM((blk, HEAD_DIM), F32)]),
        out_shape=jax.ShapeDtypeStruct((batch * seq, n_heads * HEAD_DIM), BF16),
        compiler_params=_params(("arbitrary", "arbitrary", "arbitrary"), vmem),
        name="moba_prompt",
    )(rb_flat, q, k, v, bias_tab)


def _diff_prompt_kernel(rb_ref, q_ref, k_ref, v_ref, bias_ref, lq1, lk1, lq2, lk2, g_ref, o_ref,
                        kb_sc, vb_sc, m_sc, l_sc, acc_sc, *, head0, lam_init, scale):
    t = ATT_TILE
    dh = HEAD_DIM
    h = pl.program_id(1)
    qi = pl.program_id(2)

    @pl.when(qi == 0)
    def _():
        kb_sc[...] = k_ref[...].astype(BF16)
        vb_sc[...] = v_ref[...].astype(BF16)

    qb = q_ref[...].astype(BF16)
    _softmax_init(m_sc, l_sc, acc_sc)

    def step(start, bias_add):
        v = vb_sc[pl.ds(start, t), :]
        for c in range(2):
            k = kb_sc[pl.ds(start, t), c * dh:(c + 1) * dh]
            s = lax.dot_general(qb[:, c * dh:(c + 1) * dh], k, NT_DIMS, preferred_element_type=F32) * scale + bias_add
            _softmax_step(s, v, m_sc.at[c], l_sc.at[c], acc_sc.at[c])

    far_bias = rb_ref[(head0 + h) * NUM_BUCKETS + NUM_BUCKETS - 1]

    def far(j, carry):
        step(pl.multiple_of(j * t, t), far_bias)
        return carry

    lax.fori_loop(0, qi - 1, far, 0)

    @pl.when(qi >= 1)
    def _():
        step(pl.multiple_of((qi - 1) * t, t), bias_ref[t:2 * t, :])

    step(pl.multiple_of(qi * t, t), bias_ref[0:t, :])

    lam = _diff_lambda(lq1, lk1, lq2, lk2, lam_init)
    o = acc_sc[0] / l_sc[0] - lam * (acc_sc[1] / l_sc[1])
    o_ref[...] = _sub_rmsnorm(o, g_ref, lam_init).astype(o_ref.dtype)


def _diff_prompt(q, k, v, bias_tab, rb_flat, lams, sub_g, batch, seq, head0, lam_init):
    t = ATT_TILE
    w = 2 * HEAD_DIM
    n_heads = q.shape[1] // w
    assert seq % t == 0
    nq = seq // t
    kern = functools.partial(_diff_prompt_kernel, head0=head0, lam_init=lam_init, scale=HEAD_DIM ** -0.5)
    vec = pl.BlockSpec((1, HEAD_DIM), lambda b, h, i, rb: (0, 0))
    vmem = 2 * (2 * seq * w * 4) + 2 * seq * w * 2 + 2 * 2 * t * t * 4 + 24 * t * t * 4
    tile = lambda b, h, i, rb: (b * nq + i, h)
    whole = lambda b, h, i, rb: (b, h)
    return pl.pallas_call(
        kern,
        grid_spec=pltpu.PrefetchScalarGridSpec(
            num_scalar_prefetch=1, grid=(batch, n_heads, nq),
            in_specs=[pl.BlockSpec((t, w), tile),
                      pl.BlockSpec((seq, w), whole),
                      pl.BlockSpec((seq, w), whole),
                      pl.BlockSpec((None, 2 * t, t), lambda b, h, i, rb: (head0 + h, 0, 0)),
                      vec, vec, vec, vec,
                      pl.BlockSpec((1, w), lambda b, h, i, rb: (0, 0))],
            out_specs=pl.BlockSpec((t, w), tile),
            scratch_shapes=[pltpu.VMEM((seq, w), BF16), pltpu.VMEM((seq, w), BF16),
                            pltpu.VMEM((2, t, 1), F32), pltpu.VMEM((2, t, 1), F32),
                            pltpu.VMEM((2, t, w), F32)]),
        out_shape=jax.ShapeDtypeStruct((batch * seq, n_heads * w), BF16),
        compiler_params=_params(("arbitrary", "arbitrary", "arbitrary"), vmem),
        name="diff_prompt",
    )(rb_flat, q, k, v, bias_tab, *lams, sub_g.reshape(1, w))


def _sb_logs(z, valid):
    log_beta = jnp.minimum(z, 0.0) - jnp.log(1.0 + jnp.exp(-jnp.abs(z)))
    log_keep = log_beta - z
    if valid is not None:
        log_keep = jnp.where(valid, log_keep, 0.0)
    return log_beta, log_keep


def _sum_after(log_keep, tri):
    hi = log_keep.astype(BF16)
    lo = (log_keep - hi.astype(F32)).astype(BF16)
    return jnp.dot(hi, tri, preferred_element_type=F32) + jnp.dot(lo, tri, preferred_element_type=F32)


def _tri_strict(n):
    j = np.arange(n)
    return jnp.asarray((j[:, None] > j[None, :]).astype(np.float32), dtype=BF16)


def _sb_prompt_kernel(q_ref, k_ref, v_ref, tri_ref, o_ref, kb_sc, vb_sc, c_sc, acc_sc, *, scale):
    tq, tk = SB_TQ, SB_TK
    qi = pl.program_id(2)

    @pl.when(qi == 0)
    def _():
        kb_sc[...] = k_ref[...].astype(BF16)
        vb_sc[...] = v_ref[...].astype(BF16)

    qb = q_ref[...].astype(BF16)
    tri = tri_ref[...]
    q0 = qi * tq
    c_sc[...] = jnp.zeros_like(c_sc)
    acc_sc[...] = jnp.zeros_like(acc_sc)

    def step(start, diagonal):
        k = kb_sc[pl.ds(start, tk), :]
        v = vb_sc[pl.ds(start, tk), :]
        z = lax.dot_general(qb, k, NT_DIMS, preferred_element_type=F32) * scale
        past = None
        if diagonal:
            row = lax.broadcasted_iota(jnp.int32, (tq, tk), 0) + q0
            col = lax.broadcasted_iota(jnp.int32, (tq, tk), 1) + start
            past = col < row
        log_beta, log_keep = _sb_logs(z, past)
        after = _sum_after(log_keep, tri)
        a = jnp.exp(log_beta + (after + c_sc[...]))
        if diagonal:
            a = jnp.where(past, a, 0.0)
        acc_sc[...] += jnp.dot(a.astype(BF16), v, preferred_element_type=F32)
        c_sc[...] += after[:, 0:1] + log_keep[:, 0:1]

    for d in reversed(range(tq // tk)):
        step(pl.multiple_of(q0 + d * tk, tk), True)

    def body(j, carry):
        step(pl.multiple_of(q0 - (j + 1) * tk, tk), False)
        return carry

    lax.fori_loop(0, qi * (tq // tk), body, 0)
    o_ref[...] = acc_sc[...].astype(o_ref.dtype)


def _sb_prompt(q, k, v, batch, seq):
    tq, tk = SB_TQ, SB_TK
    n_heads = q.shape[1] // HEAD_DIM
    assert seq % tq == 0 and tq % tk == 0
    nq = seq // tq
    vmem = 2 * (2 * seq * HEAD_DIM * 4) + 2 * seq * HEAD_DIM * 2 + 24 * tq * tk * 4
    tile = lambda b, h, i: (b * nq + i, h)
    whole = lambda b, h, i: (b, h)
    return pl.pallas_call(
        functools.partial(_sb_prompt_kernel, scale=HEAD_DIM ** -0.5),
        grid=(batch, n_heads, nq),
        in_specs=[pl.BlockSpec((tq, HEAD_DIM), tile),
                  pl.BlockSpec((seq, HEAD_DIM), whole),
                  pl.BlockSpec((seq, HEAD_DIM), whole),
                  pl.BlockSpec((tk, tk), lambda b, h, i: (0, 0))],
        out_specs=pl.BlockSpec((tq, HEAD_DIM), tile),
        out_shape=jax.ShapeDtypeStruct((batch * seq, n_heads * HEAD_DIM), BF16),
        scratch_shapes=[pltpu.VMEM((seq, HEAD_DIM), BF16), pltpu.VMEM((seq, HEAD_DIM), BF16),
                        pltpu.VMEM((tq, 1), F32), pltpu.VMEM((tq, HEAD_DIM), F32)],
        compiler_params=_params(("arbitrary", "arbitrary", "arbitrary"), vmem),
        name="sb_prompt",
    )(q, k, v, _tri_strict(tk))


def _first_rows(rows, n):
    s, r, w = rows.shape
    return jnp.zeros((s, n, w), rows.dtype).at[:, :r, :].set(rows)


def _head_match(shape, n_heads, row_head):
    assert n_heads & (n_heads - 1) == 0
    col = lax.broadcasted_iota(jnp.int32, shape, 1)
    return (col & (n_heads - 1)) == row_head


def _diff_sample_kernel(pt_ref, q_ref, kc_ref, vc_ref, kn_ref, vn_ref, brow_ref, lq1, lk1, lq2, lk2, g_ref,
                        o_ref, m_sc, l_sc, acc_sc, *, n_pages, n_heads, lam_init, scale):
    p = pl.program_id(1)
    rows_kv = PAGE_SIZE * n_heads

    @pl.when(p == 0)
    def _():
        _softmax_init(m_sc, l_sc, acc_sc)

    def process(k_ref, v_ref, kind):
        s = jnp.concatenate(
            [lax.dot_general(q_ref[c], k_ref[pl.ds(c, rows_kv, stride=2), :].astype(BF16), NT_DIMS,
                             preferred_element_type=F32) for c in range(2)], axis=0)
        s = s * scale + brow_ref[kind]
        row = lax.broadcasted_iota(jnp.int32, s.shape, 0)
        s = jnp.where(_head_match(s.shape, n_heads, row & (n_heads - 1)), s, NEG)
        _softmax_step(s, v_ref[...].astype(BF16), m_sc, l_sc, acc_sc)

    @pl.when(p < n_pages)
    def _():
        process(kc_ref, vc_ref, jnp.where(p == n_pages - 1, 1, 0))

    @pl.when(p == n_pages)
    def _():
        process(kn_ref, vn_ref, 2)
        lam = _diff_lambda(lq1, lk1, lq2, lk2, lam_init)
        o0 = acc_sc[0:n_heads, :] / l_sc[0:n_heads, :]
        o1 = acc_sc[n_heads:2 * n_heads, :] / l_sc[n_heads:2 * n_heads, :]
        o_ref[...] = _sub_rmsnorm(o0 - lam * o1, g_ref, lam_init)


def _diff_sample(q, k_new, v_new, cache_k, cache_v, page_table, brow, lams, sub_g, lam_init):
    s, n_heads, _, dh = q.shape
    n_pages = page_table.shape[1]
    rk, rv = PAGE_SIZE * n_heads * 2, PAGE_SIZE * n_heads
    kern = functools.partial(_diff_sample_kernel, n_pages=n_pages, n_heads=n_heads, lam_init=lam_init,
                             scale=HEAD_DIM ** -0.5)
    page = lambda i, p, pt: (pt[i, jnp.minimum(p, n_pages - 1)], 0)
    mine = lambda i, p, pt: (i, 0, 0)
    vec = pl.BlockSpec((1, HEAD_DIM), lambda i, p, pt: (0, 0))
    vmem = 2 * 4 * rk * dh * 4 + 8 * rk * dh * 4
    return pl.pallas_call(
        kern,
        grid_spec=pltpu.PrefetchScalarGridSpec(
            num_scalar_prefetch=1, grid=(s, n_pages + 1),
            in_specs=[pl.BlockSpec((None, 2, n_heads, dh), lambda i, p, pt: (i, 0, 0, 0)),
                      pl.BlockSpec((rk, dh), page),
                      pl.BlockSpec((rv, 2 * dh), page),
                      pl.BlockSpec((None, rk, dh), mine),
                      pl.BlockSpec((None, rv, 2 * dh), mine),
                      pl.BlockSpec((3, 2 * n_heads, rv), lambda i, p, pt: (0, 0, 0)),
                      vec, vec, vec, vec,
                      pl.BlockSpec((1, 2 * dh), lambda i, p, pt: (0, 0))],
            out_specs=pl.BlockSpec((None, n_heads, 2 * dh), mine),
            scratch_shapes=[pltpu.VMEM((2 * n_heads, 1), F32), pltpu.VMEM((2 * n_heads, 1), F32),
                            pltpu.VMEM((2 * n_heads, 2 * dh), F32)]),
        out_shape=jax.ShapeDtypeStruct((s, n_heads, 2 * dh), F32),
        compiler_params=_params(("arbitrary", "arbitrary"), vmem),
        name="diff_sample",
    )(page_table, q.transpose(0, 2, 1, 3).astype(BF16), cache_k, cache_v,
      _first_rows(k_new, rk), _first_rows(v_new, rv), brow, *lams, sub_g.reshape(1, 2 * dh))


def _sb_sample_kernel(pt_ref, q_ref, kc_ref, vc_ref, tri_ref, o_ref, c_sc, acc_sc, *, n_pages, n_heads, scale):
    p = pl.program_id(1)
    n_chunks = PAGE_SIZE * n_heads // LANES

    @pl.when(p == 0)
    def _():
        c_sc[...] = jnp.zeros_like(c_sc)
        acc_sc[...] = jnp.zeros_like(acc_sc)

    z = lax.dot_general(q_ref[...], kc_ref[...].astype(BF16), NT_DIMS, preferred_element_type=F32) * scale
    z3 = jnp.stack([z[:, j * LANES:(j + 1) * LANES] for j in range(n_chunks)])
    row = lax.broadcasted_iota(jnp.int32, (n_heads, LANES), 0)
    valid = _head_match((n_heads, LANES), n_heads, row)[None]
    log_beta, log_keep = _sb_logs(z3, valid)
    after = _sum_after(log_keep.reshape(n_chunks * n_heads, LANES), tri_ref[...]).reshape(n_chunks, n_heads, LANES)
    total = after[:, :, 0:1] + log_keep[:, :, 0:1]
    run = c_sc[...]
    later = [None] * n_chunks
    for j in reversed(range(n_chunks)):
        later[j] = run
        run = run + total[j]
    c_sc[...] = run
    a3 = jnp.where(valid, jnp.exp(log_beta + (after + jnp.stack(later))), 0.0)
    a = jnp.concatenate([a3[j] for j in range(n_chunks)], axis=1).astype(BF16)
    acc_sc[...] += jnp.dot(a, vc_ref[...].astype(BF16), preferred_element_type=F32)

    @pl.when(p == n_pages - 1)
    def _():
        o_ref[...] = acc_sc[...]


def _sb_sample(q, cache_k, cache_v, page_table):
    s, n_heads, dh = q.shape
    n_pages = page_table.shape[1]
    rows = PAGE_SIZE * n_heads
    assert LANES % n_heads == 0
    kern = functools.partial(_sb_sample_kernel, n_pages=n_pages, n_heads=n_heads, scale=HEAD_DIM ** -0.5)
    page = lambda i, p, pt: (pt[i, n_pages - 1 - p], 0)
    mine = lambda i, p, pt: (i, 0, 0)
    vmem = 2 * 2 * rows * dh * 4 + 4 * rows * dh * 2 + 16 * n_heads * rows * 4
    return pl.pallas_call(
        kern,
        grid_spec=pltpu.PrefetchScalarGridSpec(
            num_scalar_prefetch=1, grid=(s, n_pages),
            in_specs=[pl.BlockSpec((None, n_heads, dh), mine),
                      pl.BlockSpec((rows, dh), page),
                      pl.BlockSpec((rows, dh), page),
                      pl.BlockSpec((LANES, LANES), lambda i, p, pt: (0, 0))],
            out_specs=pl.BlockSpec((None, n_heads, dh), mine),
            scratch_shapes=[pltpu.VMEM((n_heads, 1), F32), pltpu.VMEM((n_heads, dh), F32)]),
        out_shape=jax.ShapeDtypeStruct((s, n_heads, dh), F32),
        compiler_params=_params(("arbitrary", "arbitrary"), vmem),
        name="sb_sample",
    )(page_table, q.astype(BF16), cache_k, cache_v, _tri_strict(LANES))


def _moba_kmean_kernel(pt_ref, ka_ref, kb_ref, o_ref, *, n_heads):
    shape = (PAGE_SIZE, n_heads, HEAD_DIM)
    tot = jnp.sum(ka_ref[...].reshape(shape), axis=0) + jnp.sum(kb_ref[...].reshape(shape), axis=0)
    o_ref[...] = tot * (1.0 / MOBA_BLOCK)


def _moba_kmean(cache_k, page_table, n_heads):
    assert MOBA_BLOCK == 2 * PAGE_SIZE and n_heads % 8 == 0
    s, n_pages = page_table.shape
    assert n_pages % 2 == 0
    nblk = n_pages // 2
    rows = PAGE_SIZE * n_heads
    return pl.pallas_call(
        functools.partial(_moba_kmean_kernel, n_heads=n_heads),
        grid_spec=pltpu.PrefetchScalarGridSpec(
            num_scalar_prefetch=1, grid=(s, nblk),
            in_specs=[pl.BlockSpec((rows, HEAD_DIM), lambda i, n, pt: (pt[i, 2 * n], 0)),
                      pl.BlockSpec((rows, HEAD_DIM), lambda i, n, pt: (pt[i, 2 * n + 1], 0))],
            out_specs=pl.BlockSpec((None, None, n_heads, HEAD_DIM), lambda i, n, pt: (i, n, 0, 0))),
        out_shape=jax.ShapeDtypeStruct((s, nblk, n_heads, HEAD_DIM), F32),
        compiler_params=_params(("arbitrary", "arbitrary"), 6 * rows * HEAD_DIM * 4),
        name="moba_kmean",
    )(page_table, cache_k, cache_k)


def _moba_top_kernel(km_ref, q_ref, o_ref):
    nblk = km_ref.shape[0]
    gates = jnp.sum(km_ref[...] * q_ref[...][None], axis=2, keepdims=True)
    blk = lax.broadcasted_iota(jnp.int32, gates.shape, 0).astype(F32)
    for r in range(MOBA_TOPK):
        best = jnp.max(gates, axis=0, keepdims=True)
        idx = jnp.min(jnp.where(gates == best, blk, float(nblk)), axis=0, keepdims=True)
        o_ref[r] = idx[0].astype(jnp.int32)
        gates = jnp.where(blk == idx, -jnp.inf, gates)


def _moba_top(kmean, q):
    s, nblk, n_heads, dh = kmean.shape
    assert nblk >= MOBA_TOPK
    return pl.pallas_call(
        _moba_top_kernel,
        grid=(s,),
        in_specs=[pl.BlockSpec((None, nblk, n_heads, dh), lambda i: (i, 0, 0, 0)),
                  pl.BlockSpec((None, n_heads, dh), lambda i: (i, 0, 0))],
        out_specs=pl.BlockSpec((None, MOBA_TOPK, n_heads, 1), lambda i: (i, 0, 0, 0)),
        out_shape=jax.ShapeDtypeStruct((s, MOBA_TOPK, n_heads, 1), jnp.int32),
        compiler_params=_params(("arbitrary",), 6 * nblk * n_heads * dh * 4),
        name="moba_top",
    )(kmean, q)


def _moba_attend_kernel(phys_ref, logi_ref, q_ref, kc_ref, vc_ref, kn_ref, vn_ref, brow_ref, o_ref,
                        m_sc, l_sc, acc_sc, *, n_sel, n_pages, n_heads, scale):
    i = pl.program_id(0)
    h = pl.program_id(1)
    j = pl.program_id(2)

    @pl.when(j == 0)
    def _():
        _softmax_init(m_sc, l_sc, acc_sc)

    def process(k_ref, v_ref, kind):
        s = lax.dot_general(q_ref[...], k_ref[...].astype(BF16), NT_DIMS, preferred_element_type=F32)
        s = s * scale + brow_ref[pl.ds(kind, 1), :]
        s = jnp.where(_head_match(s.shape, n_heads, h), s, NEG)
        _softmax_step(s, v_ref[...].astype(BF16), m_sc, l_sc, acc_sc)

    @pl.when(j < n_sel)
    def _():
        process(kc_ref, vc_ref, jnp.where(logi_ref[i, h, j] == n_pages - 1, 1, 0))

    @pl.when(j == n_sel)
    def _():
        process(kn_ref, vn_ref, 2)
        o_ref[...] = acc_sc[...] / l_sc[...]


def _moba_attend(q, k_new, v_new, cache_k, cache_v, phys, logical, brow, n_pages):
    s, n_heads, dh = q.shape
    n_sel = phys.shape[2]
    rows = PAGE_SIZE * n_heads
    q8 = jnp.zeros((s, n_heads, 8, dh), BF16).at[:, :, 0, :].set(q.astype(BF16))
    kern = functools.partial(_moba_attend_kernel, n_sel=n_sel, n_pages=n_pages, n_heads=n_heads,
                             scale=HEAD_DIM ** -0.5)
    page = lambda i, h, j, ph, lg: (ph[i, h, jnp.minimum(j, n_sel - 1)], 0)
    mine = lambda i, h, j, ph, lg: (i, 0, 0)
    out = pl.pallas_call(
        kern,
        grid_spec=pltpu.PrefetchScalarGridSpec(
            num_scalar_prefetch=2, grid=(s, n_heads, n_sel + 1),
            in_specs=[pl.BlockSpec((None, None, 8, dh), lambda i, h, j, ph, lg: (i, h, 0, 0)),
                      pl.BlockSpec((rows, dh), page),
                      pl.BlockSpec((rows, dh), page),
                      pl.BlockSpec((None, rows, dh), mine),
                      pl.BlockSpec((None, rows, dh), mine),
                      pl.BlockSpec((None, 8, rows), lambda i, h, j, ph, lg: (h, 0, 0))],
            out_specs=pl.BlockSpec((None, None, 8, dh), lambda i, h, j, ph, lg: (i, h, 0, 0)),
            scratch_shapes=[pltpu.VMEM((8, 1), F32), pltpu.VMEM((8, 1), F32), pltpu.VMEM((8, dh), F32)]),
        out_shape=jax.ShapeDtypeStruct((s, n_heads, 8, dh), F32),
        compiler_params=_params(("arbitrary", "arbitrary", "arbitrary"), 16 * rows * dh * 4),
        name="moba_attend",
    )(phys, logical, q8, cache_k, cache_v, _first_rows(k_new, rows), _first_rows(v_new, rows), brow)
    return out[:, :, 0, :]


def _moba_sample(q, k_new, v_new, cache_k, cache_v, page_table, brow):
    s, n_heads, dh = q.shape
    n_pages = page_table.shape[1]
    kmean = _moba_kmean(cache_k, page_table, n_heads)
    blocks = _moba_top(kmean, q)[..., 0].transpose(0, 2, 1)
    logical = (2 * blocks[..., None] + jnp.arange(2, dtype=jnp.int32)).reshape(s, n_heads, 2 * MOBA_TOPK)
    phys = jnp.take_along_axis(page_table[:, None, :], logical, axis=2)
    return _moba_attend(q, k_new, v_new, cache_k, cache_v, phys, logical, brow, n_pages)


MM_PROMPT = dict(tm=1024, tn=512, tk=4096)
MM_PROMPT_DOWN = dict(tm=512, tn=512, tk=5504)
MM_SAMPLE = dict(tm=16, tn=512, tk=16384)


def _lam_init(layer):
    return 0.8 - 0.6 * math.exp(-0.3 * layer)


def _ffn_and_ple(h, p_rows, up_gate, i, wts, mm, mm_down):
    f = _rmsnorm(h, wts['g_ffn'][i], BF16)
    act, g_rows = up_gate(f, wts['w_up'][i])
    h = _mm(act, wts['w_down'][i], extras=(h,), epilogue=_add_residual, name="down", **mm_down)
    pp = _mm(p_rows[i], wts['w_ple_proj'][i], name="ple_proj", **mm)
    a = _rmsnorm(h, wts['g_ple'][i], BF16)
    h = _mm(a, wts['w_ple_gate'][i], extras=(h, pp), epilogue=_ple_combine, name="ple_gate", **mm)
    return h, g_rows


def kernel(x_prompt, x_sample, cache_moba_k, cache_moba_v, cache_diff_k, cache_diff_v, cache_sb_k, cache_sb_v,
           state_conv, page_table, p_prompt, p_sample, rel_bias, w_in_even, w_out_even, lam_q1, lam_k1, lam_q2,
           lam_k2, diff_subln, w_in_odd, w_out_odd, g_mix, g_ffn, w_up, conv_w, conv_b, w_down, g_ple,
           w_ple_gate, w_ple_proj, g_final):
    batch, seq, d = x_prompt.shape
    dec = x_sample.shape[0]
    depth = g_mix.shape[0]
    assert depth == 2 and x_sample.shape[1] == 1
    dh = HEAD_DIM
    moba_heads = cache_moba_k.shape[3]
    diff_heads = cache_diff_k.shape[3]
    sb_heads = cache_sb_k.shape[3]
    moba_w, diff_w, sb_w = moba_heads * dh, diff_heads * 2 * dh, sb_heads * dh
    dff = conv_w.shape[2]
    n_pages = page_table.shape[1]
    m_p = batch * seq
    m_s = BF16_SUBLANES

    per_layer = lambda w: [w[i].astype(BF16) for i in range(depth)]
    wts = dict(g_ffn=g_ffn, g_ple=g_ple, w_up=per_layer(w_up), w_down=per_layer(w_down),
               w_ple_gate=per_layer(w_ple_gate), w_ple_proj=per_layer(w_ple_proj))
    cuts_e = np.cumsum([0, moba_w, moba_w, moba_w, diff_w, diff_w, diff_w])
    w_in_e = [w_in_even[0][:, lo:hi].astype(BF16) for lo, hi in zip(cuts_e[:-1], cuts_e[1:])]
    w_in_o = [w_in_odd[0][:, j * sb_w:(j + 1) * sb_w].astype(BF16) for j in range(3)]
    w_out_e_m = w_out_even[0][:moba_w].astype(BF16)
    w_out_e_d = w_out_even[0][moba_w:].astype(BF16)
    w_out_o = w_out_odd[0].astype(BF16)
    lams = tuple(v[0].reshape(1, dh) for v in (lam_q1, lam_k1, lam_q2, lam_k2))
    sub_g = diff_subln[0]
    lam0 = _lam_init(0)

    rb_flat = rel_bias.T.reshape(-1)
    bias_tab = _prompt_bias_tables(rb_flat)
    bias_rows = _sample_bias_rows(rb_flat)

    h = x_prompt.reshape(m_p, d)
    pp_rows = p_prompt.reshape(depth, m_p, -1).astype(BF16)
    tail_tm = 1024
    up_gate_p = lambda i: (lambda f, w: _up_gate_prompt(f, w, conv_w[i], conv_b[i], seq, tm=tail_tm))

    a = _rmsnorm(h, g_mix[0], BF16)
    mq, mk, mv, dq, dk, dv = [_mm(a, w, name="in_proj", **MM_PROMPT) for w in w_in_e]
    mo = _moba_prompt(mq, mk, mv, bias_tab, rb_flat, batch, seq)
    do = _diff_prompt(dq, dk, dv, bias_tab, rb_flat, lams, sub_g, batch, seq, head0=moba_heads, lam_init=lam0)
    h = _mm(mo, w_out_e_m, extras=(h,), epilogue=_add_residual, name="out_proj", **MM_PROMPT)
    h = _mm(do, w_out_e_d, extras=(h,), epilogue=_add_residual, name="out_proj", **MM_PROMPT)
    h, tail0 = _ffn_and_ple(h, pp_rows, up_gate_p(0), 0, wts, MM_PROMPT, MM_PROMPT_DOWN)

    a = _rmsnorm(h, g_mix[1], BF16)
    sq, sk, sv = [_mm(a, w, name="in_proj", **MM_PROMPT) for w in w_in_o]
    so = _sb_prompt(sq, sk, sv, batch, seq)
    h = _mm(so, w_out_o, extras=(h,), epilogue=_add_residual, name="out_proj", **MM_PROMPT)
    h, tail1 = _ffn_and_ple(h, pp_rows, up_gate_p(1), 1, wts, MM_PROMPT, MM_PROMPT_DOWN)
    y_prompt = _rmsnorm(h, g_final, F32).reshape(batch, seq, d)

    def conv_rows(tail):
        per_seq = tail.reshape(batch, seq // tail_tm, 8, dff)
        return per_seq[:, -1, 8 - (CONV_W - 1):, :]

    conv_p = jnp.stack([conv_rows(tail0), conv_rows(tail1)])
    rows5 = lambda x, *shape: x.reshape((1, batch, seq) + shape)

    pad = lambda r: jnp.zeros((m_s,) + r.shape[1:], r.dtype).at[:dec].set(r)
    hs = pad(x_sample.reshape(dec, d))
    ps_rows = jnp.stack([pad(p_sample[i].reshape(dec, -1)) for i in range(depth)]).astype(BF16)

    def up_gate_s(i):
        s0, s1 = pad(state_conv[i, :, 0]), pad(state_conv[i, :, 1])
        return lambda f, w: _up_gate_sample(f, w, s0, s1, conv_w[i], conv_b[i])

    a = _rmsnorm(hs, g_mix[0], BF16)
    mq_s, mk_s, mv_s, dq_s, dk_s, dv_s = [_mm(a, w, name="in_proj_s", **MM_SAMPLE)[:dec] for w in w_in_e]
    brow_m = jnp.repeat(bias_rows[:moba_heads], moba_heads, axis=2)
    per_map = bias_rows[moba_heads:, :3].transpose(1, 0, 2)
    brow_d = jnp.repeat(jnp.concatenate([per_map, per_map], axis=1), diff_heads, axis=2)
    heads3 = lambda x, n: x.reshape(dec, n, -1)
    mo_s = _moba_sample(heads3(mq_s, moba_heads), heads3(mk_s, moba_heads), heads3(mv_s, moba_heads),
                        cache_moba_k[0].reshape(-1, dh), cache_moba_v[0].reshape(-1, dh), page_table, brow_m)
    do_s = _diff_sample(dq_s.reshape(dec, diff_heads, 2, dh), heads3(dk_s, 2 * diff_heads), heads3(dv_s, diff_heads),
                        cache_diff_k[0].reshape(-1, dh), cache_diff_v[0].reshape(-1, 2 * dh), page_table, brow_d,
                        lams, sub_g, lam0)
    mo_s, do_s = mo_s.reshape(dec, moba_w), do_s.reshape(dec, diff_w)
    hs = _mm(pad(mo_s).astype(BF16), w_out_e_m, extras=(hs,), epilogue=_add_residual, name="out_proj_s", **MM_SAMPLE)
    hs = _mm(pad(do_s).astype(BF16), w_out_e_d, extras=(hs,), epilogue=_add_residual, name="out_proj_s", **MM_SAMPLE)
    hs, gs0 = _ffn_and_ple(hs, ps_rows, up_gate_s(0), 0, wts, MM_SAMPLE, MM_SAMPLE)

    a = _rmsnorm(hs, g_mix[1], BF16)
    sq_s, sk_s, sv_s = [_mm(a, w, name="in_proj_s", **MM_SAMPLE)[:dec] for w in w_in_o]
    so_s = _sb_sample(sq_s.reshape(dec, sb_heads, dh), cache_sb_k[0].reshape(-1, dh),
                      cache_sb_v[0].reshape(-1, dh), page_table)
    hs = _mm(pad(so_s.reshape(dec, sb_w)).astype(BF16), w_out_o, extras=(hs,), epilogue=_add_residual,
             name="out_proj_s", **MM_SAMPLE)
    hs, gs1 = _ffn_and_ple(hs, ps_rows, up_gate_s(1), 1, wts, MM_SAMPLE, MM_SAMPLE)
    y_sample = _rmsnorm(hs, g_final, F32)[:dec].reshape(dec, 1, d)

    conv_s = jnp.stack([jnp.stack([state_conv[i, :, 1], g[:dec]], axis=1) for i, g in enumerate((gs0, gs1))])

    return (y_prompt, y_sample,
            rows5(mk, moba_heads, dh), rows5(mv, moba_heads, dh),
            rows5(dk, diff_heads, 2, dh), rows5(dv, diff_heads, 2 * dh),
            rows5(sk, sb_heads, dh), rows5(sv, sb_heads, dh), conv_p,
            mk_s.reshape(1, dec, 1, moba_heads, dh), mv_s.reshape(1, dec, 1, moba_heads, dh),
            dk_s.reshape(1, dec, 1, diff_heads, 2, dh), dv_s.reshape(1, dec, 1, diff_heads, 2 * dh),
            sk_s.reshape(1, dec, 1, sb_heads, dh), sv_s.reshape(1, dec, 1, sb_heads, dh), conv_s)
```

```python
import functools
import math

import numpy as np
import jax
import jax.numpy as jnp
from jax import lax
from jax.experimental import pallas as pl
from jax.experimental.pallas import tpu as pltpu

F32 = jnp.float32
BF16 = jnp.bfloat16

HEAD_DIM = 128
MOBA_BLOCK = 256
MOBA_TOPK = 3
NUM_BUCKETS = 32
MAX_DISTANCE = 128
PAGE_SIZE = 128
CONV_W = 3
EPS = 1e-6
NEG = -1e30
ATT_TILE = 256
SB_TQ = 512
SB_TK = 256
LANES = 128
BF16_SUBLANES = 16
V7X_VMEM_BYTES = 64 * 1024 * 1024
VMEM_CAP = V7X_VMEM_BYTES - 8 * 1024 * 1024
NT_DIMS = (((1,), (1,)), ((), ()))


def _params(sem, vmem_bytes):
    limit = int(min(max(vmem_bytes * 5 // 4, 16 * 1024 * 1024), VMEM_CAP))
    return pltpu.CompilerParams(dimension_semantics=sem, vmem_limit_bytes=limit)


def _bucket_np(dist):
    n = np.maximum(dist, 0)
    max_exact = NUM_BUCKETS // 2
    nf = np.maximum(n, 1).astype(np.float32)
    large = max_exact + (np.log(nf / np.float32(max_exact)) / np.float32(math.log(MAX_DISTANCE / max_exact))
                         * np.float32(NUM_BUCKETS - max_exact)).astype(np.int32)
    return np.where(n < max_exact, n, np.minimum(large, NUM_BUCKETS - 1)).astype(np.int32)


def _bias_table_kernel(rb_ref, bkt_ref, mask_ref, o_ref):
    h = pl.program_id(0)
    bkt = bkt_ref[...]
    acc = jnp.zeros(bkt.shape, F32)
    for b in range(NUM_BUCKETS):
        acc = jnp.where(bkt == b, rb_ref[h * NUM_BUCKETS + b], acc)
    o_ref[...] = acc + mask_ref[...]


def _bias_table(rb_flat, bkt, mask):
    n_heads = rb_flat.shape[0] // NUM_BUCKETS
    r, c = bkt.shape
    return pl.pallas_call(
        _bias_table_kernel,
        grid_spec=pltpu.PrefetchScalarGridSpec(
            num_scalar_prefetch=1, grid=(n_heads,),
            in_specs=[pl.BlockSpec((r, c), lambda h, rb: (0, 0)),
                      pl.BlockSpec((r, c), lambda h, rb: (0, 0))],
            out_specs=pl.BlockSpec((None, r, c), lambda h, rb: (h, 0, 0))),
        out_shape=jax.ShapeDtypeStruct((n_heads, r, c), F32),
        compiler_params=_params(("arbitrary",), 8 * r * c * 4),
        name="bias_table",
    )(rb_flat, jnp.asarray(bkt), jnp.asarray(mask))


def _prompt_bias_tables(rb_flat):
    t = ATT_TILE
    r = np.arange(t)[:, None]
    c = np.arange(t)[None, :]
    d0 = r - c
    d1 = t + r - c
    assert int(_bucket_np(np.array([2 * t - (t - 1)]))[0]) == NUM_BUCKETS - 1
    bkt = np.concatenate([_bucket_np(d0), _bucket_np(d1)], axis=0)
    mask = np.concatenate([np.where(d0 >= 0, 0.0, NEG), np.zeros((t, t))], axis=0).astype(np.float32)
    return _bias_table(rb_flat, bkt, mask)


def _sample_bias_rows(rb_flat):
    assert int(_bucket_np(np.array([PAGE_SIZE + 1]))[0]) == NUM_BUCKETS - 1
    j = np.arange(PAGE_SIZE)
    bkt = np.zeros((8, PAGE_SIZE), np.int32)
    mask = np.zeros((8, PAGE_SIZE), np.float32)
    bkt[0] = NUM_BUCKETS - 1
    bkt[1] = _bucket_np(PAGE_SIZE - j)
    bkt[2] = 0
    mask[2, 1:] = NEG
    return _bias_table(rb_flat, bkt, mask)


def _rmsnorm_kernel(x_ref, g_ref, o_ref):
    x = x_ref[...]
    y = x * lax.rsqrt(jnp.mean(x * x, axis=-1, keepdims=True) + EPS)
    o_ref[...] = (y * g_ref[...]).astype(o_ref.dtype)


def _rmsnorm(x, g, out_dtype):
    m, d = x.shape
    tr = min(m, 256)
    assert m % tr == 0
    return pl.pallas_call(
        _rmsnorm_kernel,
        grid=(m // tr,),
        in_specs=[pl.BlockSpec((tr, d), lambda i: (i, 0)),
                  pl.BlockSpec((1, d), lambda i: (0, 0))],
        out_specs=pl.BlockSpec((tr, d), lambda i: (i, 0)),
        out_shape=jax.ShapeDtypeStruct((m, d), out_dtype),
        compiler_params=_params(("parallel",), 4 * tr * d * 4),
        name="rmsnorm",
    )(x, g.reshape(1, d))


def _mm_kernel(*refs, nk, n_extra, epilogue):
    a_ref, w_ref = refs[0], refs[1]
    extra = refs[2:2 + n_extra]
    o_ref = refs[2 + n_extra]

    def finish(acc):
        o_ref[...] = epilogue(acc, *[e[...] for e in extra]).astype(o_ref.dtype)

    if nk == 1:
        finish(jnp.dot(a_ref[...], w_ref[...], preferred_element_type=F32))
    else:
        acc_ref = refs[3 + n_extra]
        k = pl.program_id(2)

        @pl.when(k == 0)
        def _():
            acc_ref[...] = jnp.zeros_like(acc_ref)

        acc_ref[...] += jnp.dot(a_ref[...], w_ref[...], preferred_element_type=F32)

        @pl.when(k == nk - 1)
        def _():
            finish(acc_ref[...])


def _mm(a, w, *, tm, tn, tk, extras=(), epilogue=None, out_dtype=F32, name="matmul"):
    m, kdim = a.shape
    n = w.shape[1]
    tm, tn, tk = min(tm, m), min(tn, n), min(tk, kdim)
    assert m % tm == 0 and n % tn == 0 and kdim % tk == 0
    nk = kdim // tk
    if epilogue is None:
        epilogue = lambda acc: acc
    in_specs = [pl.BlockSpec((tm, tk), lambda i, j, k: (i, k)),
                pl.BlockSpec((tk, tn), lambda i, j, k: (k, j))]
    in_specs += [pl.BlockSpec((tm, tn), lambda i, j, k: (i, j)) for _ in extras]
    scratch = [pltpu.VMEM((tm, tn), F32)] if nk > 1 else []
    vmem = 2 * (tm * tk * 2 + tk * tn * 2) + (2 * len(extras) + 4) * tm * tn * 4
    return pl.pallas_call(
        functools.partial(_mm_kernel, nk=nk, n_extra=len(extras), epilogue=epilogue),
        grid=(m // tm, n // tn, nk),
        in_specs=in_specs,
        out_specs=pl.BlockSpec((tm, tn), lambda i, j, k: (i, j)),
        out_shape=jax.ShapeDtypeStruct((m, n), out_dtype),
        scratch_shapes=scratch,
        compiler_params=_params(("parallel", "parallel", "arbitrary"), vmem),
        name=name,
    )(a, w, *extras)


def _add_residual(acc, h):
    return h + acc


def _ple_combine(acc, h, pp):
    return h + jax.nn.sigmoid(acc) * pp


def _conv_taps(cw_ref, cb_ref, g2, g1, g0):
    gc = cb_ref[...] + cw_ref[0:1, :] * g2
    gc = gc + cw_ref[1:2, :] * g1
    return gc + cw_ref[2:3, :] * g0


def _up_gate_prompt_kernel(a_ref, ap_ref, wg_ref, wu_ref, cw_ref, cb_ref, act_ref, tail_ref, *, seq):
    tm = a_ref.shape[0]
    a = a_ref[...]
    g = jnp.dot(a, wg_ref[...], preferred_element_type=F32)
    u = jnp.dot(a, wu_ref[...], preferred_element_type=F32)
    prev = jnp.dot(ap_ref[...], wg_ref[...], preferred_element_type=F32)
    last = BF16_SUBLANES - 1
    row = lax.broadcasted_iota(jnp.int32, (tm, 1), 0)
    pos = (pl.program_id(0) * tm + row) % seq
    g1 = jnp.where(row == 0, prev[last:last + 1, :], pltpu.roll(g, 1, axis=0))
    g1 = jnp.where(pos >= 1, g1, 0.0)
    g2 = jnp.where(row == 0, prev[last - 1:last, :],
                   jnp.where(row == 1, prev[last:last + 1, :], pltpu.roll(g, 2, axis=0)))
    g2 = jnp.where(pos >= 2, g2, 0.0)
    gc = _conv_taps(cw_ref, cb_ref, g2, g1, g)
    act_ref[...] = (jax.nn.silu(gc) * u).astype(act_ref.dtype)
    tail_ref[...] = g[tm - 8:tm, :]


def _up_gate_prompt(a, w_up, conv_w, conv_b, seq, tm=1024, tn=256):
    m, d = a.shape
    dff = conv_w.shape[1]
    assert m % tm == 0 and dff % tn == 0 and seq % tm == 0
    nj = dff // tn
    per16 = tm // BF16_SUBLANES
    vmem = 2 * (tm * d * 2 + BF16_SUBLANES * d * 2 + 2 * d * tn * 2) + 12 * tm * tn * 4
    return pl.pallas_call(
        functools.partial(_up_gate_prompt_kernel, seq=seq),
        grid=(m // tm, nj),
        in_specs=[pl.BlockSpec((tm, d), lambda i, j: (i, 0)),
                  pl.BlockSpec((BF16_SUBLANES, d), lambda i, j: (jnp.maximum(i * per16 - 1, 0), 0)),
                  pl.BlockSpec((d, tn), lambda i, j: (0, j)),
                  pl.BlockSpec((d, tn), lambda i, j: (0, j + nj)),
                  pl.BlockSpec((CONV_W, tn), lambda i, j: (0, j)),
                  pl.BlockSpec((1, tn), lambda i, j: (0, j))],
        out_specs=[pl.BlockSpec((tm, tn), lambda i, j: (i, j)),
                   pl.BlockSpec((8, tn), lambda i, j: (i, j))],
        out_shape=[jax.ShapeDtypeStruct((m, dff), BF16),
                   jax.ShapeDtypeStruct((m // tm * 8, dff), F32)],
        compiler_params=_params(("parallel", "parallel"), vmem),
        name="up_gate_prompt",
    )(a, a, w_up, w_up, conv_w, conv_b.reshape(1, dff))


def _up_gate_sample_kernel(a_ref, wg_ref, wu_ref, s0_ref, s1_ref, cw_ref, cb_ref, act_ref, g_ref):
    a = a_ref[...]
    g = jnp.dot(a, wg_ref[...], preferred_element_type=F32)
    u = jnp.dot(a, wu_ref[...], preferred_element_type=F32)
    gc = _conv_taps(cw_ref, cb_ref, s0_ref[...], s1_ref[...], g)
    act_ref[...] = (jax.nn.silu(gc) * u).astype(act_ref.dtype)
    g_ref[...] = g


def _up_gate_sample(a, w_up, s0, s1, conv_w, conv_b, tn=256):
    m, d = a.shape
    dff = conv_w.shape[1]
    assert dff % tn == 0
    nj = dff // tn
    col = lambda j: (0, j)
    return pl.pallas_call(
        _up_gate_sample_kernel,
        grid=(nj,),
        in_specs=[pl.BlockSpec((m, d), lambda j: (0, 0)),
                  pl.BlockSpec((d, tn), col),
                  pl.BlockSpec((d, tn), lambda j: (0, j + nj)),
                  pl.BlockSpec((m, tn), col), pl.BlockSpec((m, tn), col),
                  pl.BlockSpec((CONV_W, tn), col), pl.BlockSpec((1, tn), col)],
        out_specs=[pl.BlockSpec((m, tn), col), pl.BlockSpec((m, tn), col)],
        out_shape=[jax.ShapeDtypeStruct((m, dff), BF16), jax.ShapeDtypeStruct((m, dff), F32)],
        compiler_params=_params(("parallel",), 2 * (m * d * 2 + 2 * d * tn * 2) + 16 * m * tn * 4),
        name="up_gate_sample",
    )(a, w_up, w_up, s0, s1, conv_w, conv_b.reshape(1, dff))


def _softmax_step(s, v, m_ref, l_ref, acc_ref):
    m_prev = m_ref[...]
    m_new = jnp.maximum(m_prev, jnp.max(s, axis=1, keepdims=True))
    alpha = jnp.exp(m_prev - m_new)
    p = jnp.exp(s - m_new)
    l_ref[...] = alpha * l_ref[...] + jnp.sum(p, axis=1, keepdims=True)
    acc_ref[...] = alpha * acc_ref[...] + jnp.dot(p.astype(BF16), v, preferred_element_type=F32)
    m_ref[...] = m_new


def _softmax_tiles(scores, values):
    m = functools.reduce(jnp.maximum, [jnp.max(s, axis=1, keepdims=True) for s in scores])
    probs = [jnp.exp(s - m) for s in scores]
    l = functools.reduce(jnp.add, [jnp.sum(p, axis=1, keepdims=True) for p in probs])
    o = functools.reduce(jnp.add, [jnp.dot(p.astype(BF16), v, preferred_element_type=F32)
                                   for p, v in zip(probs, values)])
    return o, l


def _softmax_init(m_ref, l_ref, acc_ref):
    m_ref[...] = jnp.full_like(m_ref, NEG)
    l_ref[...] = jnp.zeros_like(l_ref)
    acc_ref[...] = jnp.zeros_like(acc_ref)


def _diff_lambda(lq1, lk1, lq2, lk2, lam_init):
    return (jnp.exp(jnp.sum(lq1[...] * lk1[...], axis=1, keepdims=True))
            - jnp.exp(jnp.sum(lq2[...] * lk2[...], axis=1, keepdims=True)) + lam_init)


def _sub_rmsnorm(o, g_ref, lam_init):
    return o * lax.rsqrt(jnp.mean(o * o, axis=-1, keepdims=True) + EPS) * (g_ref[...] * (1.0 - lam_init))


def _moba_prompt_kernel(rb_ref, q_ref, k_ref, v_ref, bias_ref, o_ref,
                        kb_sc, vb_sc, km_sc, *, nb, scale):
    blk = MOBA_BLOCK
    h = pl.program_id(1)
    qi = pl.program_id(2)

    @pl.when(qi == 0)
    def _():
        kb_sc[...] = k_ref[...].astype(BF16)
        vb_sc[...] = v_ref[...].astype(BF16)
        km_sc[...] = jnp.zeros_like(km_sc)
        for n in range(nb):
            km_sc[n:n + 1, :] = jnp.sum(k_ref[n * blk:(n + 1) * blk, :], axis=0, keepdims=True) * (1.0 / blk)

    q32 = q_ref[...]
    qb = q32.astype(BF16)
    gate = lax.dot_general(q32, km_sc[...], NT_DIMS, precision=lax.Precision.HIGHEST,
                           preferred_element_type=F32)
    lane = lax.broadcasted_iota(jnp.int32, gate.shape, 1)
    gm = jnp.where(lane < qi, gate, -jnp.inf)
    rank = jnp.zeros(gate.shape, jnp.int32)
    for m in range(nb - 1):
        col = gm[:, m:m + 1]
        ahead = (col > gm) | ((col == gm) & (m < lane))
        rank = rank + ahead.astype(jnp.int32)
    addm = jnp.where((rank < MOBA_TOPK) & (lane < qi), 0.0, NEG)

    far_bias = rb_ref[h * NUM_BUCKETS + NUM_BUCKETS - 1]

    def attend(c):
        scores, values = [], []
        for n in range(c + 1):
            rows = slice(n * blk, (n + 1) * blk)
            s = lax.dot_general(qb, kb_sc[rows, :], NT_DIMS, preferred_element_type=F32) * scale
            if n == c:
                s = s + bias_ref[0:blk, :]
            elif n == c - 1:
                s = s + bias_ref[blk:2 * blk, :] + addm[:, n:n + 1]
            else:
                s = s + (far_bias + addm[:, n:n + 1])
            scores.append(s)
            values.append(vb_sc[rows, :])
        o, l = _softmax_tiles(scores, values)
        o_ref[...] = (o / l).astype(o_ref.dtype)

    for c in range(nb):
        pl.when(qi == c)(functools.partial(attend, c))


def _moba_prompt(q, k, v, bias_tab, rb_flat, batch, seq):
    blk = MOBA_BLOCK
    n_heads = q.shape[1] // HEAD_DIM
    assert seq % blk == 0 and blk == ATT_TILE
    nb = seq // blk
    assert nb <= HEAD_DIM
    kern = functools.partial(_moba_prompt_kernel, nb=nb, scale=HEAD_DIM ** -0.5)
    vmem = 2 * (2 * seq * HEAD_DIM * 4) + 2 * seq * HEAD_DIM * 2 + 2 * 2 * blk * blk * 4 + 4 * blk * seq * 4
    tile = lambda b, h, i, rb: (b * nb + i, h)
    whole = lambda b, h, i, rb: (b, h)
    return pl.pallas_call(
        kern,
        grid_spec=pltpu.PrefetchScalarGridSpec(
            num_scalar_prefetch=1, grid=(batch, n_heads, nb),
            in_specs=[pl.BlockSpec((blk, HEAD_DIM), tile),
                      pl.BlockSpec((seq, HEAD_DIM), whole),
                      pl.BlockSpec((seq, HEAD_DIM), whole),
                      pl.BlockSpec((None, 2 * blk, blk), lambda b, h, i, rb: (h, 0, 0))],
            out_specs=pl.BlockSpec((blk, HEAD_DIM), tile),
            scratch_shapes=[pltpu.VMEM((seq, HEAD_DIM), BF16), pltpu.VMEM((seq, HEAD_DIM), BF16),
                            pltpu.VMEM((HEAD_DIM, HEAD_DIM), F32)]),
        out_shape=jax.ShapeDtypeStruct((batch * seq, n_heads * HEAD_DIM), BF16),
        compiler_params=_params(("arbitrary", "arbitrary", "arbitrary"), vmem),
        name="moba_prompt",
    )(rb_flat, q, k, v, bias_tab)


def _diff_prompt_kernel(rb_ref, q_ref, k_ref, v_ref, bias_ref, lq1, lk1, lq2, lk2, g_ref, o_ref,
                        kb_sc, vb_sc, *, nq, head0, lam_init, scale):
    t = ATT_TILE
    dh = HEAD_DIM
    h = pl.program_id(1)
    qi = pl.program_id(2)

    @pl.when(qi == 0)
    def _():
        kb_sc[...] = k_ref[...].astype(BF16)
        vb_sc[...] = v_ref[...].astype(BF16)

    qb = q_ref[...].astype(BF16)
    far_bias = rb_ref[(head0 + h) * NUM_BUCKETS + NUM_BUCKETS - 1]

    def attend(c):
        values = [vb_sc[n * t:(n + 1) * t, :] for n in range(c + 1)]
        outs = []
        for mp in range(2):
            scores = []
            for n in range(c + 1):
                k = kb_sc[n * t:(n + 1) * t, mp * dh:(mp + 1) * dh]
                s = lax.dot_general(qb[:, mp * dh:(mp + 1) * dh], k, NT_DIMS, preferred_element_type=F32) * scale
                if n == c:
                    s = s + bias_ref[0:t, :]
                elif n == c - 1:
                    s = s + bias_ref[t:2 * t, :]
                else:
                    s = s + far_bias
                scores.append(s)
            o, l = _softmax_tiles(scores, values)
            outs.append(o / l)
        lam = _diff_lambda(lq1, lk1, lq2, lk2, lam_init)
        o_ref[...] = _sub_rmsnorm(outs[0] - lam * outs[1], g_ref, lam_init).astype(o_ref.dtype)

    for c in range(nq):
        pl.when(qi == c)(functools.partial(attend, c))


def _diff_prompt(q, k, v, bias_tab, rb_flat, lams, sub_g, batch, seq, head0, lam_init):
    t = ATT_TILE
    w = 2 * HEAD_DIM
    n_heads = q.shape[1] // w
    assert seq % t == 0
    nq = seq // t
    kern = functools.partial(_diff_prompt_kernel, nq=nq, head0=head0, lam_init=lam_init, scale=HEAD_DIM ** -0.5)
    vec = pl.BlockSpec((1, HEAD_DIM), lambda b, h, i, rb: (0, 0))
    vmem = 2 * (2 * seq * w * 4) + 2 * seq * w * 2 + 2 * 2 * t * t * 4 + 8 * t * seq * 4
    tile = lambda b, h, i, rb: (b * nq + i, h)
    whole = lambda b, h, i, rb: (b, h)
    return pl.pallas_call(
        kern,
        grid_spec=pltpu.PrefetchScalarGridSpec(
            num_scalar_prefetch=1, grid=(batch, n_heads, nq),
            in_specs=[pl.BlockSpec((t, w), tile),
                      pl.BlockSpec((seq, w), whole),
                      pl.BlockSpec((seq, w), whole),
                      pl.BlockSpec((None, 2 * t, t), lambda b, h, i, rb: (head0 + h, 0, 0)),
                      vec, vec, vec, vec,
                      pl.BlockSpec((1, w), lambda b, h, i, rb: (0, 0))],
            out_specs=pl.BlockSpec((t, w), tile),
            scratch_shapes=[pltpu.VMEM((seq, w), BF16), pltpu.VMEM((seq, w), BF16)]),
        out_shape=jax.ShapeDtypeStruct((batch * seq, n_heads * w), BF16),
        compiler_params=_params(("arbitrary", "arbitrary", "arbitrary"), vmem),
        name="diff_prompt",
    )(rb_flat, q, k, v, bias_tab, *lams, sub_g.reshape(1, w))


def _sb_logs(z, valid):
    log_beta = jnp.minimum(z, 0.0) - jnp.log(1.0 + jnp.exp(-jnp.abs(z)))
    log_keep = log_beta - z
    if valid is not None:
        log_keep = jnp.where(valid, log_keep, 0.0)
    return log_beta, log_keep


def _sum_after(log_keep, tri):
    hi = log_keep.astype(BF16)
    lo = (log_keep - hi.astype(F32)).astype(BF16)
    return jnp.dot(hi, tri, preferred_element_type=F32) + jnp.dot(lo, tri, preferred_element_type=F32)


def _tri_strict(n):
    j = np.arange(n)
    return jnp.asarray((j[:, None] > j[None, :]).astype(np.float32), dtype=BF16)


def _sb_prompt_kernel(q_ref, k_ref, v_ref, tri_ref, o_ref, kb_sc, vb_sc, *, nq, scale):
    tq, tk = SB_TQ, SB_TK
    qi = pl.program_id(2)

    @pl.when(qi == 0)
    def _():
        kb_sc[...] = k_ref[...].astype(BF16)
        vb_sc[...] = v_ref[...].astype(BF16)

    qb = q_ref[...].astype(BF16)
    tri = tri_ref[...]

    def attend(c):
        q0 = c * tq
        n_tiles = (q0 + tq) // tk
        tiles = []
        for n in range(n_tiles):
            start = n * tk
            z = lax.dot_general(qb, kb_sc[start:start + tk, :], NT_DIMS, preferred_element_type=F32) * scale
            past = None
            if start + tk > q0:
                row = lax.broadcasted_iota(jnp.int32, (tq, tk), 0) + q0
                col = lax.broadcasted_iota(jnp.int32, (tq, tk), 1) + start
                past = col < row
            log_beta, log_keep = _sb_logs(z, past)
            after = _sum_after(log_keep, tri)
            tiles.append((log_beta, after, after[:, 0:1] + log_keep[:, 0:1], past))
        carry = jnp.zeros((tq, 1), F32)
        acc = jnp.zeros((tq, HEAD_DIM), F32)
        for n in reversed(range(n_tiles)):
            log_beta, after, total, past = tiles[n]
            a = jnp.exp(log_beta + (after + carry))
            if past is not None:
                a = jnp.where(past, a, 0.0)
            acc = acc + jnp.dot(a.astype(BF16), vb_sc[n * tk:(n + 1) * tk, :], preferred_element_type=F32)
            carry = carry + total
        o_ref[...] = acc.astype(o_ref.dtype)

    for c in range(nq):
        pl.when(qi == c)(functools.partial(attend, c))


def _sb_prompt(q, k, v, batch, seq):
    tq, tk = SB_TQ, SB_TK
    n_heads = q.shape[1] // HEAD_DIM
    assert seq % tq == 0 and tq % tk == 0
    nq = seq // tq
    vmem = 2 * (2 * seq * HEAD_DIM * 4) + 2 * seq * HEAD_DIM * 2 + 6 * tq * seq * 4
    tile = lambda b, h, i: (b * nq + i, h)
    whole = lambda b, h, i: (b, h)
    return pl.pallas_call(
        functools.partial(_sb_prompt_kernel, nq=nq, scale=HEAD_DIM ** -0.5),
        grid=(batch, n_heads, nq),
        in_specs=[pl.BlockSpec((tq, HEAD_DIM), tile),
                  pl.BlockSpec((seq, HEAD_DIM), whole),
                  pl.BlockSpec((seq, HEAD_DIM), whole),
                  pl.BlockSpec((tk, tk), lambda b, h, i: (0, 0))],
        out_specs=pl.BlockSpec((tq, HEAD_DIM), tile),
        out_shape=jax.ShapeDtypeStruct((batch * seq, n_heads * HEAD_DIM), BF16),
        scratch_shapes=[pltpu.VMEM((seq, HEAD_DIM), BF16), pltpu.VMEM((seq, HEAD_DIM), BF16)],
        compiler_params=_params(("arbitrary", "arbitrary", "arbitrary"), vmem),
        name="sb_prompt",
    )(q, k, v, _tri_strict(tk))


def _first_rows(rows, n):
    s, r, w = rows.shape
    return jnp.zeros((s, n, w), rows.dtype).at[:, :r, :].set(rows)


def _head_match(shape, n_heads, row_head):
    assert n_heads & (n_heads - 1) == 0
    col = lax.broadcasted_iota(jnp.int32, shape, 1)
    return (col & (n_heads - 1)) == row_head


RING = 3


def _page_ring(t, n_total, row_starts, srcs, bufs, sems):
    def copies(u):
        slot = u % RING
        return [pltpu.make_async_copy(src.at[pl.ds(start, buf.shape[1]), :], buf.at[slot], sems.at[k, slot])
                for k, (src, buf, start) in enumerate(zip(srcs, bufs, row_starts(u)))]

    @pl.when(t == 0)
    def _():
        for u in range(min(RING - 1, n_total)):
            for c in copies(u):
                c.start()

    @pl.when(t + (RING - 1) < n_total)
    def _():
        for c in copies(t + (RING - 1)):
            c.start()

    for c in copies(t):
        c.wait()
    return [buf.at[t % RING] for buf in bufs]


def _diff_sample_kernel(pt_ref, q_ref, kc_hbm, vc_hbm, kn_ref, vn_ref, brow_ref, lq1, lk1, lq2, lk2, g_ref,
                        o_ref, m_sc, l_sc, acc_sc, kbuf, vbuf, sems, *, n_seq, n_pages, n_heads, lam_init, scale):
    i = pl.program_id(0)
    p = pl.program_id(1)
    rows_kv = PAGE_SIZE * n_heads

    @pl.when(p == 0)
    def _():
        _softmax_init(m_sc, l_sc, acc_sc)

    def row_starts(u):
        page = pt_ref[u // n_pages, u % n_pages]
        return (pl.multiple_of(page * (2 * rows_kv), 2 * rows_kv), pl.multiple_of(page * rows_kv, rows_kv))

    def process(k_ref, v_ref, kind):
        s = jnp.concatenate(
            [lax.dot_general(q_ref[c], k_ref[pl.ds(c, rows_kv, stride=2), :].astype(BF16), NT_DIMS,
                             preferred_element_type=F32) for c in range(2)], axis=0)
        s = s * scale + brow_ref[kind]
        row = lax.broadcasted_iota(jnp.int32, s.shape, 0)
        s = jnp.where(_head_match(s.shape, n_heads, row & (n_heads - 1)), s, NEG)
        _softmax_step(s, v_ref[...].astype(BF16), m_sc, l_sc, acc_sc)

    @pl.when(p < n_pages)
    def _():
        k_view, v_view = _page_ring(i * n_pages + p, n_seq * n_pages, row_starts, (kc_hbm, vc_hbm),
                                    (kbuf, vbuf), sems)
        process(k_view, v_view, jnp.where(p == n_pages - 1, 1, 0))

    @pl.when(p == n_pages)
    def _():
        process(kn_ref, vn_ref, 2)
        lam = _diff_lambda(lq1, lk1, lq2, lk2, lam_init)
        o0 = acc_sc[0:n_heads, :] / l_sc[0:n_heads, :]
        o1 = acc_sc[n_heads:2 * n_heads, :] / l_sc[n_heads:2 * n_heads, :]
        o_ref[...] = _sub_rmsnorm(o0 - lam * o1, g_ref, lam_init)


def _diff_sample(q, k_new, v_new, cache_k, cache_v, page_table, brow, lams, sub_g, lam_init):
    s, n_heads, _, dh = q.shape
    n_pages = page_table.shape[1]
    rk, rv = PAGE_SIZE * n_heads * 2, PAGE_SIZE * n_heads
    kern = functools.partial(_diff_sample_kernel, n_seq=s, n_pages=n_pages, n_heads=n_heads, lam_init=lam_init,
                             scale=HEAD_DIM ** -0.5)
    mine = lambda i, p, pt: (i, 0, 0)
    vec = pl.BlockSpec((1, HEAD_DIM), lambda i, p, pt: (0, 0))
    vmem = (2 * RING + 4) * rk * dh * 4 + 8 * rk * dh * 4
    return pl.pallas_call(
        kern,
        grid_spec=pltpu.PrefetchScalarGridSpec(
            num_scalar_prefetch=1, grid=(s, n_pages + 1),
            in_specs=[pl.BlockSpec((None, 2, n_heads, dh), lambda i, p, pt: (i, 0, 0, 0)),
                      pl.BlockSpec(memory_space=pltpu.HBM),
                      pl.BlockSpec(memory_space=pltpu.HBM),
                      pl.BlockSpec((None, rk, dh), mine),
                      pl.BlockSpec((None, rv, 2 * dh), mine),
                      pl.BlockSpec((3, 2 * n_heads, rv), lambda i, p, pt: (0, 0, 0)),
                      vec, vec, vec, vec,
                      pl.BlockSpec((1, 2 * dh), lambda i, p, pt: (0, 0))],
            out_specs=pl.BlockSpec((None, n_heads, 2 * dh), mine),
            scratch_shapes=[pltpu.VMEM((2 * n_heads, 1), F32), pltpu.VMEM((2 * n_heads, 1), F32),
                            pltpu.VMEM((2 * n_heads, 2 * dh), F32),
                            pltpu.VMEM((RING, rk, dh), F32), pltpu.VMEM((RING, rv, 2 * dh), F32),
                            pltpu.SemaphoreType.DMA((2, RING))]),
        out_shape=jax.ShapeDtypeStruct((s, n_heads, 2 * dh), F32),
        compiler_params=_params(("arbitrary", "arbitrary"), vmem),
        name="diff_sample",
    )(page_table, q.transpose(0, 2, 1, 3).astype(BF16), cache_k, cache_v,
      _first_rows(k_new, rk), _first_rows(v_new, rv), brow, *lams, sub_g.reshape(1, 2 * dh))


def _sb_sample_kernel(pt_ref, q_ref, kc_hbm, vc_hbm, tri_ref, o_ref, c_sc, acc_sc, kbuf, vbuf, sems,
                      *, n_seq, n_pages, n_heads, scale):
    i = pl.program_id(0)
    p = pl.program_id(1)
    rows = PAGE_SIZE * n_heads
    n_chunks = rows // LANES

    @pl.when(p == 0)
    def _():
        c_sc[...] = jnp.zeros_like(c_sc)
        acc_sc[...] = jnp.zeros_like(acc_sc)

    def row_starts(u):
        start = pl.multiple_of(pt_ref[u // n_pages, n_pages - 1 - u % n_pages] * rows, rows)
        return (start, start)

    kc_ref, vc_ref = _page_ring(i * n_pages + p, n_seq * n_pages, row_starts, (kc_hbm, vc_hbm), (kbuf, vbuf), sems)
    z = lax.dot_general(q_ref[...], kc_ref[...].astype(BF16), NT_DIMS, preferred_element_type=F32) * scale
    z3 = jnp.stack([z[:, j * LANES:(j + 1) * LANES] for j in range(n_chunks)])
    row = lax.broadcasted_iota(jnp.int32, (n_heads, LANES), 0)
    valid = _head_match((n_heads, LANES), n_heads, row)[None]
    log_beta, log_keep = _sb_logs(z3, valid)
    after = _sum_after(log_keep.reshape(n_chunks * n_heads, LANES), tri_ref[...]).reshape(n_chunks, n_heads, LANES)
    total = after[:, :, 0:1] + log_keep[:, :, 0:1]
    run = c_sc[...]
    later = [None] * n_chunks
    for j in reversed(range(n_chunks)):
        later[j] = run
        run = run + total[j]
    c_sc[...] = run
    a3 = jnp.where(valid, jnp.exp(log_beta + (after + jnp.stack(later))), 0.0)
    a = jnp.concatenate([a3[j] for j in range(n_chunks)], axis=1).astype(BF16)
    acc_sc[...] += jnp.dot(a, vc_ref[...].astype(BF16), preferred_element_type=F32)

    @pl.when(p == n_pages - 1)
    def _():
        o_ref[...] = acc_sc[...]


def _sb_sample(q, cache_k, cache_v, page_table):
    s, n_heads, dh = q.shape
    n_pages = page_table.shape[1]
    rows = PAGE_SIZE * n_heads
    assert LANES % n_heads == 0
    kern = functools.partial(_sb_sample_kernel, n_seq=s, n_pages=n_pages, n_heads=n_heads, scale=HEAD_DIM ** -0.5)
    mine = lambda i, p, pt: (i, 0, 0)
    vmem = 2 * RING * rows * dh * 4 + 4 * rows * dh * 2 + 16 * n_heads * rows * 4
    return pl.pallas_call(
        kern,
        grid_spec=pltpu.PrefetchScalarGridSpec(
            num_scalar_prefetch=1, grid=(s, n_pages),
            in_specs=[pl.BlockSpec((None, n_heads, dh), mine),
                      pl.BlockSpec(memory_space=pltpu.HBM),
                      pl.BlockSpec(memory_space=pltpu.HBM),
                      pl.BlockSpec((LANES, LANES), lambda i, p, pt: (0, 0))],
            out_specs=pl.BlockSpec((None, n_heads, dh), mine),
            scratch_shapes=[pltpu.VMEM((n_heads, 1), F32), pltpu.VMEM((n_heads, dh), F32),
                            pltpu.VMEM((RING, rows, dh), F32), pltpu.VMEM((RING, rows, dh), F32),
                            pltpu.SemaphoreType.DMA((2, RING))]),
        out_shape=jax.ShapeDtypeStruct((s, n_heads, dh), F32),
        compiler_params=_params(("arbitrary", "arbitrary"), vmem),
        name="sb_sample",
    )(page_table, q.astype(BF16), cache_k, cache_v, _tri_strict(LANES))


def _moba_kmean_kernel(pt_ref, k_hbm, o_ref, abuf, bbuf, sems, *, n_seq, n_blocks, n_heads):
    rows = PAGE_SIZE * n_heads
    n = pl.program_id(1)

    def row_starts(u):
        i, blk = u // n_blocks, u % n_blocks
        return (pl.multiple_of(pt_ref[i, 2 * blk] * rows, rows), pl.multiple_of(pt_ref[i, 2 * blk + 1] * rows, rows))

    ka_ref, kb_ref = _page_ring(pl.program_id(0) * n_blocks + n, n_seq * n_blocks, row_starts, (k_hbm, k_hbm),
                                (abuf, bbuf), sems)
    shape = (PAGE_SIZE, n_heads, HEAD_DIM)
    tot = jnp.sum(ka_ref[...].reshape(shape), axis=0) + jnp.sum(kb_ref[...].reshape(shape), axis=0)
    o_ref[...] = tot * (1.0 / MOBA_BLOCK)


def _moba_kmean(cache_k, page_table, n_heads):
    assert MOBA_BLOCK == 2 * PAGE_SIZE and n_heads % 8 == 0
    s, n_pages = page_table.shape
    assert n_pages % 2 == 0
    nblk = n_pages // 2
    rows = PAGE_SIZE * n_heads
    return pl.pallas_call(
        functools.partial(_moba_kmean_kernel, n_seq=s, n_blocks=nblk, n_heads=n_heads),
        grid_spec=pltpu.PrefetchScalarGridSpec(
            num_scalar_prefetch=1, grid=(s, nblk),
            in_specs=[pl.BlockSpec(memory_space=pltpu.HBM)],
            out_specs=pl.BlockSpec((None, None, n_heads, HEAD_DIM), lambda i, n, pt: (i, n, 0, 0)),
            scratch_shapes=[pltpu.VMEM((RING, rows, HEAD_DIM), F32), pltpu.VMEM((RING, rows, HEAD_DIM), F32),
                            pltpu.SemaphoreType.DMA((2, RING))]),
        out_shape=jax.ShapeDtypeStruct((s, nblk, n_heads, HEAD_DIM), F32),
        compiler_params=_params(("arbitrary", "arbitrary"), (2 * RING + 4) * rows * HEAD_DIM * 4),
        name="moba_kmean",
    )(page_table, cache_k)


def _moba_top_kernel(km_ref, q_ref, o_ref):
    nblk = km_ref.shape[0]
    gates = jnp.sum(km_ref[...] * q_ref[...][None], axis=2, keepdims=True)
    blk = lax.broadcasted_iota(jnp.int32, gates.shape, 0).astype(F32)
    for r in range(MOBA_TOPK):
        best = jnp.max(gates, axis=0, keepdims=True)
        idx = jnp.min(jnp.where(gates == best, blk, float(nblk)), axis=0, keepdims=True)
        o_ref[r] = idx[0].astype(jnp.int32)
        gates = jnp.where(blk == idx, -jnp.inf, gates)


def _moba_top(kmean, q):
    s, nblk, n_heads, dh = kmean.shape
    assert nblk >= MOBA_TOPK
    return pl.pallas_call(
        _moba_top_kernel,
        grid=(s,),
        in_specs=[pl.BlockSpec((None, nblk, n_heads, dh), lambda i: (i, 0, 0, 0)),
                  pl.BlockSpec((None, n_heads, dh), lambda i: (i, 0, 0))],
        out_specs=pl.BlockSpec((None, MOBA_TOPK, n_heads, 1), lambda i: (i, 0, 0, 0)),
        out_shape=jax.ShapeDtypeStruct((s, MOBA_TOPK, n_heads, 1), jnp.int32),
        compiler_params=_params(("arbitrary",), 6 * nblk * n_heads * dh * 4),
        name="moba_top",
    )(kmean, q)


def _moba_attend_kernel(phys_ref, logi_ref, q_ref, kc_hbm, vc_hbm, kn_ref, vn_ref, brow_ref, o_ref,
                        m_sc, l_sc, acc_sc, kbuf, vbuf, sems, *, n_seq, n_sel, n_pages, n_heads, scale):
    h = pl.program_id(1)
    j = pl.program_id(2)
    rows = PAGE_SIZE * n_heads
    t = (pl.program_id(0) * n_heads + h) * n_sel + j

    @pl.when(j == 0)
    def _():
        _softmax_init(m_sc, l_sc, acc_sc)

    def process(k_ref, v_ref, kind):
        s = lax.dot_general(q_ref[...], k_ref[...].astype(BF16), NT_DIMS, preferred_element_type=F32)
        s = s * scale + brow_ref[pl.ds(kind, 1), :]
        s = jnp.where(_head_match(s.shape, n_heads, h), s, NEG)
        _softmax_step(s, v_ref[...].astype(BF16), m_sc, l_sc, acc_sc)

    def row_starts(u):
        start = pl.multiple_of(phys_ref[u] * rows, rows)
        return (start, start)

    @pl.when(j < n_sel)
    def _():
        k_view, v_view = _page_ring(t, n_seq * n_heads * n_sel, row_starts, (kc_hbm, vc_hbm), (kbuf, vbuf), sems)
        process(k_view, v_view, jnp.where(logi_ref[t] == n_pages - 1, 1, 0))

    @pl.when(j == n_sel)
    def _():
        process(kn_ref, vn_ref, 2)
        o_ref[...] = acc_sc[...] / l_sc[...]


def _moba_attend(q, k_new, v_new, cache_k, cache_v, phys, logical, brow, n_pages):
    s, n_heads, dh = q.shape
    n_sel = phys.shape[2]
    rows = PAGE_SIZE * n_heads
    q8 = jnp.zeros((s, n_heads, 8, dh), BF16).at[:, :, 0, :].set(q.astype(BF16))
    kern = functools.partial(_moba_attend_kernel, n_seq=s, n_sel=n_sel, n_pages=n_pages, n_heads=n_heads,
                             scale=HEAD_DIM ** -0.5)
    mine = lambda i, h, j, ph, lg: (i, 0, 0)
    out = pl.pallas_call(
        kern,
        grid_spec=pltpu.PrefetchScalarGridSpec(
            num_scalar_prefetch=2, grid=(s, n_heads, n_sel + 1),
            in_specs=[pl.BlockSpec((None, None, 8, dh), lambda i, h, j, ph, lg: (i, h, 0, 0)),
                      pl.BlockSpec(memory_space=pltpu.HBM),
                      pl.BlockSpec(memory_space=pltpu.HBM),
                      pl.BlockSpec((None, rows, dh), mine),
                      pl.BlockSpec((None, rows, dh), mine),
                      pl.BlockSpec((None, 8, rows), lambda i, h, j, ph, lg: (h, 0, 0))],
            out_specs=pl.BlockSpec((None, None, 8, dh), lambda i, h, j, ph, lg: (i, h, 0, 0)),
            scratch_shapes=[pltpu.VMEM((8, 1), F32), pltpu.VMEM((8, 1), F32), pltpu.VMEM((8, dh), F32),
                            pltpu.VMEM((RING, rows, dh), F32), pltpu.VMEM((RING, rows, dh), F32),
                            pltpu.SemaphoreType.DMA((2, RING))]),
        out_shape=jax.ShapeDtypeStruct((s, n_heads, 8, dh), F32),
        compiler_params=_params(("arbitrary", "arbitrary", "arbitrary"), (2 * RING + 10) * rows * dh * 4),
        name="moba_attend",
    )(phys.reshape(-1), logical.reshape(-1), q8, cache_k, cache_v, _first_rows(k_new, rows),
      _first_rows(v_new, rows), brow)
    return out[:, :, 0, :]


def _moba_sample(q, k_new, v_new, cache_k, cache_v, page_table, brow):
    s, n_heads, dh = q.shape
    n_pages = page_table.shape[1]
    kmean = _moba_kmean(cache_k, page_table, n_heads)
    blocks = _moba_top(kmean, q)[..., 0].transpose(0, 2, 1)
    logical = (2 * blocks[..., None] + jnp.arange(2, dtype=jnp.int32)).reshape(s, n_heads, 2 * MOBA_TOPK)
    phys = jnp.take_along_axis(page_table[:, None, :], logical, axis=2)
    return _moba_attend(q, k_new, v_new, cache_k, cache_v, phys, logical, brow, n_pages)


MM_PROMPT = dict(tm=1024, tn=512, tk=4096)
MM_PROMPT_DOWN = dict(tm=1024, tn=512, tk=5504)
MM_SAMPLE = dict(tm=16, tn=512, tk=16384)


def _lam_init(layer):
    return 0.8 - 0.6 * math.exp(-0.3 * layer)


def _ffn_and_ple(h, p_rows, up_gate, i, wts, mm, mm_down):
    f = _rmsnorm(h, wts['g_ffn'][i], BF16)
    act, g_rows = up_gate(f, wts['w_up'][i])
    h = _mm(act, wts['w_down'][i], extras=(h,), epilogue=_add_residual, name="down", **mm_down)
    pp = _mm(p_rows[i], wts['w_ple_proj'][i], name="ple_proj", **mm)
    a = _rmsnorm(h, wts['g_ple'][i], BF16)
    h = _mm(a, wts['w_ple_gate'][i], extras=(h, pp), epilogue=_ple_combine, name="ple_gate", **mm)
    return h, g_rows


def kernel(x_prompt, x_sample, cache_moba_k, cache_moba_v, cache_diff_k, cache_diff_v, cache_sb_k, cache_sb_v,
           state_conv, page_table, p_prompt, p_sample, rel_bias, w_in_even, w_out_even, lam_q1, lam_k1, lam_q2,
           lam_k2, diff_subln, w_in_odd, w_out_odd, g_mix, g_ffn, w_up, conv_w, conv_b, w_down, g_ple,
           w_ple_gate, w_ple_proj, g_final):
    batch, seq, d = x_prompt.shape
    dec = x_sample.shape[0]
    depth = g_mix.shape[0]
    assert depth == 2 and x_sample.shape[1] == 1
    dh = HEAD_DIM
    moba_heads = cache_moba_k.shape[3]
    diff_heads = cache_diff_k.shape[3]
    sb_heads = cache_sb_k.shape[3]
    moba_w, diff_w, sb_w = moba_heads * dh, diff_heads * 2 * dh, sb_heads * dh
    dff = conv_w.shape[2]
    n_pages = page_table.shape[1]
    m_p = batch * seq
    m_s = BF16_SUBLANES

    per_layer = lambda w: [w[i].astype(BF16) for i in range(depth)]
    wts = dict(g_ffn=g_ffn, g_ple=g_ple, w_up=per_layer(w_up), w_down=per_layer(w_down),
               w_ple_gate=per_layer(w_ple_gate), w_ple_proj=per_layer(w_ple_proj))
    cuts_e = np.cumsum([0, moba_w, moba_w, moba_w, diff_w, diff_w, diff_w])
    w_in_e = [w_in_even[0][:, lo:hi].astype(BF16) for lo, hi in zip(cuts_e[:-1], cuts_e[1:])]
    w_in_o = [w_in_odd[0][:, j * sb_w:(j + 1) * sb_w].astype(BF16) for j in range(3)]
    w_out_e_m = w_out_even[0][:moba_w].astype(BF16)
    w_out_e_d = w_out_even[0][moba_w:].astype(BF16)
    w_out_o = w_out_odd[0].astype(BF16)
    lams = tuple(v[0].reshape(1, dh) for v in (lam_q1, lam_k1, lam_q2, lam_k2))
    sub_g = diff_subln[0]
    lam0 = _lam_init(0)

    rb_flat = rel_bias.T.reshape(-1)
    bias_tab = _prompt_bias_tables(rb_flat)
    bias_rows = _sample_bias_rows(rb_flat)

    h = x_prompt.reshape(m_p, d)
    pp_rows = p_prompt.reshape(depth, m_p, -1).astype(BF16)
    tail_tm = 1024
    up_gate_p = lambda i: (lambda f, w: _up_gate_prompt(f, w, conv_w[i], conv_b[i], seq, tm=tail_tm))

    a = _rmsnorm(h, g_mix[0], BF16)
    mq, mk, mv, dq, dk, dv = [_mm(a, w, name="in_proj", **MM_PROMPT) for w in w_in_e]
    mo = _moba_prompt(mq, mk, mv, bias_tab, rb_flat, batch, seq)
    do = _diff_prompt(dq, dk, dv, bias_tab, rb_flat, lams, sub_g, batch, seq, head0=moba_heads, lam_init=lam0)
    h = _mm(mo, w_out_e_m, extras=(h,), epilogue=_add_residual, name="out_proj", **MM_PROMPT)
    h = _mm(do, w_out_e_d, extras=(h,), epilogue=_add_residual, name="out_proj", **MM_PROMPT)
    h, tail0 = _ffn_and_ple(h, pp_rows, up_gate_p(0), 0, wts, MM_PROMPT, MM_PROMPT_DOWN)

    a = _rmsnorm(h, g_mix[1], BF16)
    sq, sk, sv = [_mm(a, w, name="in_proj", **MM_PROMPT) for w in w_in_o]
    so = _sb_prompt(sq, sk, sv, batch, seq)
    h = _mm(so, w_out_o, extras=(h,), epilogue=_add_residual, name="out_proj", **MM_PROMPT)
    h, tail1 = _ffn_and_ple(h, pp_rows, up_gate_p(1), 1, wts, MM_PROMPT, MM_PROMPT_DOWN)
    y_prompt = _rmsnorm(h, g_final, F32).reshape(batch, seq, d)

    def conv_rows(tail):
        per_seq = tail.reshape(batch, seq // tail_tm, 8, dff)
        return per_seq[:, -1, 8 - (CONV_W - 1):, :]

    conv_p = jnp.stack([conv_rows(tail0), conv_rows(tail1)])
    rows5 = lambda x, *shape: x.reshape((1, batch, seq) + shape)

    pad = lambda r: jnp.zeros((m_s,) + r.shape[1:], r.dtype).at[:dec].set(r)
    hs = pad(x_sample.reshape(dec, d))
    ps_rows = jnp.stack([pad(p_sample[i].reshape(dec, -1)) for i in range(depth)]).astype(BF16)

    def up_gate_s(i):
        s0, s1 = pad(state_conv[i, :, 0]), pad(state_conv[i, :, 1])
        return lambda f, w: _up_gate_sample(f, w, s0, s1, conv_w[i], conv_b[i])

    a = _rmsnorm(hs, g_mix[0], BF16)
    mq_s, mk_s, mv_s, dq_s, dk_s, dv_s = [_mm(a, w, name="in_proj_s", **MM_SAMPLE)[:dec] for w in w_in_e]
    brow_m = jnp.repeat(bias_rows[:moba_heads], moba_heads, axis=2)
    per_map = bias_rows[moba_heads:, :3].transpose(1, 0, 2)
    brow_d = jnp.repeat(jnp.concatenate([per_map, per_map], axis=1), diff_heads, axis=2)
    heads3 = lambda x, n: x.reshape(dec, n, -1)
    mo_s = _moba_sample(heads3(mq_s, moba_heads), heads3(mk_s, moba_heads), heads3(mv_s, moba_heads),
                        cache_moba_k[0].reshape(-1, dh), cache_moba_v[0].reshape(-1, dh), page_table, brow_m)
    do_s = _diff_sample(dq_s.reshape(dec, diff_heads, 2, dh), heads3(dk_s, 2 * diff_heads), heads3(dv_s, diff_heads),
                        cache_diff_k[0].reshape(-1, dh), cache_diff_v[0].reshape(-1, 2 * dh), page_table, brow_d,
                        lams, sub_g, lam0)
    mo_s, do_s = mo_s.reshape(dec, moba_w), do_s.reshape(dec, diff_w)
    hs = _mm(pad(mo_s).astype(BF16), w_out_e_m, extras=(hs,), epilogue=_add_residual, name="out_proj_s", **MM_SAMPLE)
    hs = _mm(pad(do_s).astype(BF16), w_out_e_d, extras=(hs,), epilogue=_add_residual, name="out_proj_s", **MM_SAMPLE)
    hs, gs0 = _ffn_and_ple(hs, ps_rows, up_gate_s(0), 0, wts, MM_SAMPLE, MM_SAMPLE)

    a = _rmsnorm(hs, g_mix[1], BF16)
    sq_s, sk_s, sv_s = [_mm(a, w, name="in_proj_s", **MM_SAMPLE)[:dec] for w in w_in_o]
    so_s = _sb_sample(sq_s.reshape(dec, sb_heads, dh), cache_sb_k[0].reshape(-1, dh),
                      cache_sb_v[0].reshape(-1, dh), page_table)
    hs = _mm(pad(so_s.reshape(dec, sb_w)).astype(BF16), w_out_o, extras=(hs,), epilogue=_add_residual,
             name="out_proj_s", **MM_SAMPLE)
    hs, gs1 = _ffn_and_ple(hs, ps_rows, up_gate_s(1), 1, wts, MM_SAMPLE, MM_SAMPLE)
    y_sample = _rmsnorm(hs, g_final, F32)[:dec].reshape(dec, 1, d)

    conv_s = jnp.stack([jnp.stack([state_conv[i, :, 1], g[:dec]], axis=1) for i, g in enumerate((gs0, gs1))])

    return (y_prompt, y_sample,
            rows5(mk, moba_heads, dh), rows5(mv, moba_heads, dh),
            rows5(dk, diff_heads, 2, dh), rows5(dv, diff_heads, 2 * dh),
            rows5(sk, sb_heads, dh), rows5(sv, sb_heads, dh), conv_p,
            mk_s.reshape(1, dec, 1, moba_heads, dh), mv_s.reshape(1, dec, 1, moba_heads, dh),
            dk_s.reshape(1, dec, 1, diff_heads, 2, dh), dv_s.reshape(1, dec, 1, diff_heads, 2 * dh),
            sk_s.reshape(1, dec, 1, sb_heads, dh), sv_s.reshape(1, dec, 1, sb_heads, dh), conv_s)
```

```python
import functools
import math

import numpy as np
import jax
import jax.numpy as jnp
from jax import lax
from jax.experimental import pallas as pl
from jax.experimental.pallas import tpu as pltpu

F32 = jnp.float32
BF16 = jnp.bfloat16

HEAD_DIM = 128
MOBA_BLOCK = 256
MOBA_TOPK = 3
NUM_BUCKETS = 32
MAX_DISTANCE = 128
PAGE_SIZE = 128
CONV_W = 3
EPS = 1e-6
NEG = -1e30
ATT_TILE = 256
SB_TQ = 512
SB_TK = 256
UP_ROWS = 256
LANES = 128
BF16_SUBLANES = 16
V7X_VMEM_BYTES = 64 * 1024 * 1024
VMEM_CAP = V7X_VMEM_BYTES - 8 * 1024 * 1024
NT_DIMS = (((1,), (1,)), ((), ()))


def _params(sem, vmem_bytes):
    limit = int(min(max(vmem_bytes * 5 // 4, 16 * 1024 * 1024), VMEM_CAP))
    return pltpu.CompilerParams(dimension_semantics=sem, vmem_limit_bytes=limit)


def _bucket_np(dist):
    n = np.maximum(dist, 0)
    max_exact = NUM_BUCKETS // 2
    nf = np.maximum(n, 1).astype(np.float32)
    large = max_exact + (np.log(nf / np.float32(max_exact)) / np.float32(math.log(MAX_DISTANCE / max_exact))
                         * np.float32(NUM_BUCKETS - max_exact)).astype(np.int32)
    return np.where(n < max_exact, n, np.minimum(large, NUM_BUCKETS - 1)).astype(np.int32)


def _bias_table_kernel(rb_ref, bkt_ref, mask_ref, o_ref):
    h = pl.program_id(0)
    bkt = bkt_ref[...]
    acc = jnp.zeros(bkt.shape, F32)
    for b in range(NUM_BUCKETS):
        acc = jnp.where(bkt == b, rb_ref[h * NUM_BUCKETS + b], acc)
    o_ref[...] = acc + mask_ref[...]


def _bias_table(rb_flat, bkt, mask):
    n_heads = rb_flat.shape[0] // NUM_BUCKETS
    r, c = bkt.shape
    return pl.pallas_call(
        _bias_table_kernel,
        grid_spec=pltpu.PrefetchScalarGridSpec(
            num_scalar_prefetch=1, grid=(n_heads,),
            in_specs=[pl.BlockSpec((r, c), lambda h, rb: (0, 0)),
                      pl.BlockSpec((r, c), lambda h, rb: (0, 0))],
            out_specs=pl.BlockSpec((None, r, c), lambda h, rb: (h, 0, 0))),
        out_shape=jax.ShapeDtypeStruct((n_heads, r, c), F32),
        compiler_params=_params(("arbitrary",), 8 * r * c * 4),
        name="bias_table",
    )(rb_flat, jnp.asarray(bkt), jnp.asarray(mask))


def _prompt_bias_tables(rb_flat):
    t = ATT_TILE
    r = np.arange(t)[:, None]
    c = np.arange(t)[None, :]
    d0 = r - c
    d1 = t + r - c
    assert int(_bucket_np(np.array([2 * t - (t - 1)]))[0]) == NUM_BUCKETS - 1
    bkt = np.concatenate([_bucket_np(d0), _bucket_np(d1)], axis=0)
    mask = np.concatenate([np.where(d0 >= 0, 0.0, NEG), np.zeros((t, t))], axis=0).astype(np.float32)
    return _bias_table(rb_flat, bkt, mask)


def _sample_bias_rows(rb_flat):
    assert int(_bucket_np(np.array([PAGE_SIZE + 1]))[0]) == NUM_BUCKETS - 1
    j = np.arange(PAGE_SIZE)
    bkt = np.zeros((8, PAGE_SIZE), np.int32)
    mask = np.zeros((8, PAGE_SIZE), np.float32)
    bkt[0] = NUM_BUCKETS - 1
    bkt[1] = _bucket_np(PAGE_SIZE - j)
    bkt[2] = 0
    mask[2, 1:] = NEG
    return _bias_table(rb_flat, bkt, mask)


def _rmsnorm_kernel(x_ref, g_ref, o_ref):
    x = x_ref[...]
    y = x * lax.rsqrt(jnp.mean(x * x, axis=-1, keepdims=True) + EPS)
    o_ref[...] = (y * g_ref[...]).astype(o_ref.dtype)


def _rmsnorm(x, g, out_dtype):
    m, d = x.shape
    tr = min(m, 256)
    assert m % tr == 0
    return pl.pallas_call(
        _rmsnorm_kernel,
        grid=(m // tr,),
        in_specs=[pl.BlockSpec((tr, d), lambda i: (i, 0)),
                  pl.BlockSpec((1, d), lambda i: (0, 0))],
        out_specs=pl.BlockSpec((tr, d), lambda i: (i, 0)),
        out_shape=jax.ShapeDtypeStruct((m, d), out_dtype),
        compiler_params=_params(("parallel",), 4 * tr * d * 4),
        name="rmsnorm",
    )(x, g.reshape(1, d))


def _mm_kernel(*refs, nk, n_extra, epilogue):
    a_ref, w_ref = refs[0], refs[1]
    extra = refs[2:2 + n_extra]
    o_ref = refs[2 + n_extra]

    def finish(acc):
        o_ref[...] = epilogue(acc, *[e[...] for e in extra]).astype(o_ref.dtype)

    if nk == 1:
        finish(jnp.dot(a_ref[...], w_ref[...], preferred_element_type=F32))
    else:
        acc_ref = refs[3 + n_extra]
        k = pl.program_id(2)

        @pl.when(k == 0)
        def _():
            acc_ref[...] = jnp.zeros_like(acc_ref)

        acc_ref[...] += jnp.dot(a_ref[...], w_ref[...], preferred_element_type=F32)

        @pl.when(k == nk - 1)
        def _():
            finish(acc_ref[...])


def _mm(a, w, *, tm, tn, tk, extras=(), epilogue=None, out_dtype=F32, name="matmul"):
    m, kdim = a.shape
    n = w.shape[1]
    tm, tn, tk = min(tm, m), min(tn, n), min(tk, kdim)
    assert m % tm == 0 and n % tn == 0 and kdim % tk == 0
    nk = kdim // tk
    if epilogue is None:
        epilogue = lambda acc: acc
    in_specs = [pl.BlockSpec((tm, tk), lambda i, j, k: (i, k)),
                pl.BlockSpec((tk, tn), lambda i, j, k: (k, j))]
    in_specs += [pl.BlockSpec((tm, tn), lambda i, j, k: (i, j)) for _ in extras]
    scratch = [pltpu.VMEM((tm, tn), F32)] if nk > 1 else []
    vmem = 2 * (tm * tk * 2 + tk * tn * 2) + (2 * len(extras) + 4) * tm * tn * 4
    return pl.pallas_call(
        functools.partial(_mm_kernel, nk=nk, n_extra=len(extras), epilogue=epilogue),
        grid=(m // tm, n // tn, nk),
        in_specs=in_specs,
        out_specs=pl.BlockSpec((tm, tn), lambda i, j, k: (i, j)),
        out_shape=jax.ShapeDtypeStruct((m, n), out_dtype),
        scratch_shapes=scratch,
        compiler_params=_params(("parallel", "parallel", "arbitrary"), vmem),
        name=name,
    )(a, w, *extras)


def _add_residual(acc, h):
    return h + acc


def _ple_combine(acc, h, pp):
    return h + jax.nn.sigmoid(acc) * pp


def _conv_taps(cw_ref, cb_ref, g2, g1, g0):
    gc = cb_ref[...] + cw_ref[0:1, :] * g2
    gc = gc + cw_ref[1:2, :] * g1
    return gc + cw_ref[2:3, :] * g0


def _up_gate_prompt_kernel(a_ref, ap_ref, wg_ref, wu_ref, cw_ref, cb_ref, act_ref, tail_ref, *, seq):
    tm = a_ref.shape[0]
    wg = wg_ref[...].astype(BF16)
    wu = wu_ref[...].astype(BF16)
    before = jnp.dot(ap_ref[...], wg, preferred_element_type=F32)[BF16_SUBLANES - 2:BF16_SUBLANES, :]
    row = lax.broadcasted_iota(jnp.int32, (UP_ROWS, 1), 0)
    for r0 in range(0, tm, UP_ROWS):
        a = a_ref[r0:r0 + UP_ROWS, :]
        g = jnp.dot(a, wg, preferred_element_type=F32)
        u = jnp.dot(a, wu, preferred_element_type=F32)
        pos = (pl.program_id(0) * tm + r0 + row) % seq
        g1 = jnp.where(row == 0, before[1:2, :], pltpu.roll(g, 1, axis=0))
        g1 = jnp.where(pos >= 1, g1, 0.0)
        g2 = jnp.where(row == 0, before[0:1, :], jnp.where(row == 1, before[1:2, :], pltpu.roll(g, 2, axis=0)))
        g2 = jnp.where(pos >= 2, g2, 0.0)
        gc = _conv_taps(cw_ref, cb_ref, g2, g1, g)
        act_ref[r0:r0 + UP_ROWS, :] = (jax.nn.silu(gc) * u).astype(act_ref.dtype)
        before = g[UP_ROWS - 2:UP_ROWS, :]
        if r0 + UP_ROWS == tm:
            tail_ref[...] = g[UP_ROWS - 8:UP_ROWS, :]


def _up_gate_prompt(a, w_up, layer, conv_w, conv_b, seq, tm=1024, tn=256):
    m, d = a.shape
    dff = conv_w.shape[1]
    assert m % tm == 0 and dff % tn == 0 and seq % tm == 0 and tm % UP_ROWS == 0
    nj = dff // tn
    per16 = tm // BF16_SUBLANES
    vmem = (2 * (tm * d * 2 + BF16_SUBLANES * d * 2 + 2 * d * tn * 4) + 2 * d * tn * 2
            + 2 * tm * tn * 2 + 12 * UP_ROWS * tn * 4)
    return pl.pallas_call(
        functools.partial(_up_gate_prompt_kernel, seq=seq),
        grid=(m // tm, nj),
        in_specs=[pl.BlockSpec((tm, d), lambda i, j: (i, 0)),
                  pl.BlockSpec((BF16_SUBLANES, d), lambda i, j: (jnp.maximum(i * per16 - 1, 0), 0)),
                  pl.BlockSpec((None, d, tn), lambda i, j: (layer, 0, j)),
                  pl.BlockSpec((None, d, tn), lambda i, j: (layer, 0, j + nj)),
                  pl.BlockSpec((CONV_W, tn), lambda i, j: (0, j)),
                  pl.BlockSpec((1, tn), lambda i, j: (0, j))],
        out_specs=[pl.BlockSpec((tm, tn), lambda i, j: (i, j)),
                   pl.BlockSpec((8, tn), lambda i, j: (i, j))],
        out_shape=[jax.ShapeDtypeStruct((m, dff), BF16),
                   jax.ShapeDtypeStruct((m // tm * 8, dff), F32)],
        compiler_params=_params(("parallel", "parallel"), vmem),
        name="up_gate_prompt",
    )(a, a, w_up, w_up, conv_w, conv_b.reshape(1, dff))


def _up_gate_sample_kernel(a_ref, wg_ref, wu_ref, s0_ref, s1_ref, cw_ref, cb_ref, act_ref, g_ref):
    a = a_ref[...]
    g = jnp.dot(a, wg_ref[...].astype(BF16), preferred_element_type=F32)
    u = jnp.dot(a, wu_ref[...].astype(BF16), preferred_element_type=F32)
    gc = _conv_taps(cw_ref, cb_ref, s0_ref[...], s1_ref[...], g)
    act_ref[...] = (jax.nn.silu(gc) * u).astype(act_ref.dtype)
    g_ref[...] = g


def _up_gate_sample(a, w_up, layer, s0, s1, conv_w, conv_b, tn=256):
    m, d = a.shape
    dff = conv_w.shape[1]
    assert dff % tn == 0
    nj = dff // tn
    col = lambda j: (0, j)
    return pl.pallas_call(
        _up_gate_sample_kernel,
        grid=(nj,),
        in_specs=[pl.BlockSpec((m, d), lambda j: (0, 0)),
                  pl.BlockSpec((None, d, tn), lambda j: (layer, 0, j)),
                  pl.BlockSpec((None, d, tn), lambda j: (layer, 0, j + nj)),
                  pl.BlockSpec((m, tn), col), pl.BlockSpec((m, tn), col),
                  pl.BlockSpec((CONV_W, tn), col), pl.BlockSpec((1, tn), col)],
        out_specs=[pl.BlockSpec((m, tn), col), pl.BlockSpec((m, tn), col)],
        out_shape=[jax.ShapeDtypeStruct((m, dff), BF16), jax.ShapeDtypeStruct((m, dff), F32)],
        compiler_params=_params(("parallel",), 2 * (m * d * 2 + 2 * d * tn * 4) + 2 * d * tn * 2 + 16 * m * tn * 4),
        name="up_gate_sample",
    )(a, w_up, w_up, s0, s1, conv_w, conv_b.reshape(1, dff))


def _softmax_step(s, v, m_ref, l_ref, acc_ref):
    m_prev = m_ref[...]
    m_new = jnp.maximum(m_prev, jnp.max(s, axis=1, keepdims=True))
    alpha = jnp.exp(m_prev - m_new)
    p = jnp.exp(s - m_new)
    l_ref[...] = alpha * l_ref[...] + jnp.sum(p, axis=1, keepdims=True)
    acc_ref[...] = alpha * acc_ref[...] + jnp.dot(p.astype(BF16), v, preferred_element_type=F32)
    m_ref[...] = m_new


def _softmax_tiles(scores, values):
    m = functools.reduce(jnp.maximum, [jnp.max(s, axis=1, keepdims=True) for s in scores])
    probs = [jnp.exp(s - m) for s in scores]
    l = functools.reduce(jnp.add, [jnp.sum(p, axis=1, keepdims=True) for p in probs])
    o = functools.reduce(jnp.add, [jnp.dot(p.astype(BF16), v, preferred_element_type=F32)
                                   for p, v in zip(probs, values)])
    return o, l


def _softmax_init(m_ref, l_ref, acc_ref):
    m_ref[...] = jnp.full_like(m_ref, NEG)
    l_ref[...] = jnp.zeros_like(l_ref)
    acc_ref[...] = jnp.zeros_like(acc_ref)


def _diff_lambda(lq1, lk1, lq2, lk2, lam_init):
    return (jnp.exp(jnp.sum(lq1[...] * lk1[...], axis=1, keepdims=True))
            - jnp.exp(jnp.sum(lq2[...] * lk2[...], axis=1, keepdims=True)) + lam_init)


def _sub_rmsnorm(o, g_ref, lam_init):
    return o * lax.rsqrt(jnp.mean(o * o, axis=-1, keepdims=True) + EPS) * (g_ref[...] * (1.0 - lam_init))


def _moba_prompt_kernel(rb_ref, q_ref, k_ref, v_ref, bias_ref, o_ref,
                        kb_sc, vb_sc, km_sc, *, nb, scale):
    blk = MOBA_BLOCK
    h = pl.program_id(1)
    qi = pl.program_id(2)

    @pl.when(qi == 0)
    def _():
        kb_sc[:, 0:HEAD_DIM] = k_ref[...].astype(BF16)
        row_blk = lax.broadcasted_iota(jnp.int32, (k_ref.shape[0], HEAD_DIM), 0) // blk
        lane = lax.broadcasted_iota(jnp.int32, (k_ref.shape[0], HEAD_DIM), 1)
        kb_sc[:, HEAD_DIM:2 * HEAD_DIM] = jnp.where(row_blk == lane, 1.0, 0.0).astype(BF16)
        vb_sc[...] = v_ref[...].astype(BF16)
        km_sc[...] = jnp.zeros_like(km_sc)
        for n in range(nb):
            km_sc[n:n + 1, :] = jnp.sum(k_ref[n * blk:(n + 1) * blk, :], axis=0, keepdims=True) * (1.0 / blk)

    q32 = q_ref[...]
    gate = lax.dot_general(km_sc[...], q32, NT_DIMS, precision=lax.Precision.HIGHEST,
                           preferred_element_type=F32)
    g8 = gate[0:8, :]
    blk_id = lax.broadcasted_iota(jnp.int32, g8.shape, 0)
    gm = jnp.where(blk_id < qi, g8, -jnp.inf)
    rank = jnp.zeros(g8.shape, jnp.int32)
    for m in range(nb - 1):
        other = gm[m:m + 1, :]
        ahead = (other > gm) | ((other == gm) & (m < blk_id))
        rank = rank + ahead.astype(jnp.int32)
    drop = jnp.where((blk_id < qi) & (rank >= MOBA_TOPK), NEG / scale, 0.0)
    drop = jnp.concatenate([drop, jnp.zeros((HEAD_DIM - 8, blk), F32)], axis=0).T
    qb = jnp.concatenate([q32.astype(BF16), drop.astype(BF16)], axis=1)

    far_bias = rb_ref[h * NUM_BUCKETS + NUM_BUCKETS - 1]

    def attend(c):
        scores, values = [], []
        for n in range(c + 1):
            rows = slice(n * blk, (n + 1) * blk)
            s = lax.dot_general(qb, kb_sc[rows, :], NT_DIMS, preferred_element_type=F32) * scale
            if n == c:
                s = s + bias_ref[0:blk, :]
            elif n == c - 1:
                s = s + bias_ref[blk:2 * blk, :]
            else:
                s = s + far_bias
            scores.append(s)
            values.append(vb_sc[rows, :])
        o, l = _softmax_tiles(scores, values)
        o_ref[...] = (o / l).astype(o_ref.dtype)

    for c in range(nb):
        pl.when(qi == c)(functools.partial(attend, c))


def _moba_prompt(q, k, v, bias_tab, rb_flat, batch, seq):
    blk = MOBA_BLOCK
    n_heads = q.shape[1] // HEAD_DIM
    assert seq % blk == 0 and blk == ATT_TILE
    nb = seq // blk
    assert nb <= 8
    kern = functools.partial(_moba_prompt_kernel, nb=nb, scale=HEAD_DIM ** -0.5)
    vmem = 2 * (2 * seq * HEAD_DIM * 4) + 2 * seq * HEAD_DIM * 2 + 2 * 2 * blk * blk * 4 + 4 * blk * seq * 4
    tile = lambda b, h, i, rb: (b * nb + i, h)
    whole = lambda b, h, i, rb: (b, h)
    return pl.pallas_call(
        kern,
        grid_spec=pltpu.PrefetchScalarGridSpec(
            num_scalar_prefetch=1, grid=(batch, n_heads, nb),
            in_specs=[pl.BlockSpec((blk, HEAD_DIM), tile),
                      pl.BlockSpec((seq, HEAD_DIM), whole),
                      pl.BlockSpec((seq, HEAD_DIM), whole),
                      pl.BlockSpec((None, 2 * blk, blk), lambda b, h, i, rb: (h, 0, 0))],
            out_specs=pl.BlockSpec((blk, HEAD_DIM), tile),
            scratch_shapes=[pltpu.VMEM((seq, 2 * HEAD_DIM), BF16), pltpu.VMEM((seq, HEAD_DIM), BF16),
                            pltpu.VMEM((HEAD_DIM, HEAD_DIM), F32)]),
        out_shape=jax.ShapeDtypeStruct((batch * seq, n_heads * HEAD_DIM), BF16),
        compiler_params=_params(("arbitrary", "arbitrary", "arbitrary"), vmem),
        name="moba_prompt",
    )(rb_flat, q, k, v, bias_tab)


def _diff_prompt_kernel(rb_ref, q_ref, k_ref, v_ref, bias_ref, lq1, lk1, lq2, lk2, g_ref, o_ref,
                        kb_sc, vb_sc, *, nq, head0, lam_init, scale):
    t = ATT_TILE
    dh = HEAD_DIM
    h = pl.program_id(1)
    qi = pl.program_id(2)

    @pl.when(qi == 0)
    def _():
        kb_sc[...] = k_ref[...].astype(BF16)
        vb_sc[...] = v_ref[...].astype(BF16)

    qb = q_ref[...].astype(BF16)
    far_bias = rb_ref[(head0 + h) * NUM_BUCKETS + NUM_BUCKETS - 1]

    def attend(c):
        values = [vb_sc[n * t:(n + 1) * t, :] for n in range(c + 1)]
        outs = []
        for mp in range(2):
            scores = []
            for n in range(c + 1):
                k = kb_sc[n * t:(n + 1) * t, mp * dh:(mp + 1) * dh]
                s = lax.dot_general(qb[:, mp * dh:(mp + 1) * dh], k, NT_DIMS, preferred_element_type=F32) * scale
                if n == c:
                    s = s + bias_ref[0:t, :]
                elif n == c - 1:
                    s = s + bias_ref[t:2 * t, :]
                else:
                    s = s + far_bias
                scores.append(s)
            o, l = _softmax_tiles(scores, values)
            outs.append(o / l)
        lam = _diff_lambda(lq1, lk1, lq2, lk2, lam_init)
        o_ref[...] = _sub_rmsnorm(outs[0] - lam * outs[1], g_ref, lam_init).astype(o_ref.dtype)

    for c in range(nq):
        pl.when(qi == c)(functools.partial(attend, c))


def _diff_prompt(q, k, v, bias_tab, rb_flat, lams, sub_g, batch, seq, head0, lam_init):
    t = ATT_TILE
    w = 2 * HEAD_DIM
    n_heads = q.shape[1] // w
    assert seq % t == 0
    nq = seq // t
    kern = functools.partial(_diff_prompt_kernel, nq=nq, head0=head0, lam_init=lam_init, scale=HEAD_DIM ** -0.5)
    vec = pl.BlockSpec((1, HEAD_DIM), lambda b, h, i, rb: (0, 0))
    vmem = 2 * (2 * seq * w * 4) + 2 * seq * w * 2 + 2 * 2 * t * t * 4 + 8 * t * seq * 4
    tile = lambda b, h, i, rb: (b * nq + i, h)
    whole = lambda b, h, i, rb: (b, h)
    return pl.pallas_call(
        kern,
        grid_spec=pltpu.PrefetchScalarGridSpec(
            num_scalar_prefetch=1, grid=(batch, n_heads, nq),
            in_specs=[pl.BlockSpec((t, w), tile),
                      pl.BlockSpec((seq, w), whole),
                      pl.BlockSpec((seq, w), whole),
                      pl.BlockSpec((None, 2 * t, t), lambda b, h, i, rb: (head0 + h, 0, 0)),
                      vec, vec, vec, vec,
                      pl.BlockSpec((1, w), lambda b, h, i, rb: (0, 0))],
            out_specs=pl.BlockSpec((t, w), tile),
            scratch_shapes=[pltpu.VMEM((seq, w), BF16), pltpu.VMEM((seq, w), BF16)]),
        out_shape=jax.ShapeDtypeStruct((batch * seq, n_heads * w), BF16),
        compiler_params=_params(("arbitrary", "arbitrary", "arbitrary"), vmem),
        name="diff_prompt",
    )(rb_flat, q, k, v, bias_tab, *lams, sub_g.reshape(1, w))


def _sb_logs(z, valid):
    log_beta = jnp.minimum(z, 0.0) - jnp.log(1.0 + jnp.exp(-jnp.abs(z)))
    log_keep = log_beta - z
    if valid is not None:
        log_keep = jnp.where(valid, log_keep, 0.0)
    return log_beta, log_keep


def _sum_after(log_keep, tri):
    hi = log_keep.astype(BF16)
    lo = (log_keep - hi.astype(F32)).astype(BF16)
    return jnp.dot(hi, tri, preferred_element_type=F32) + jnp.dot(lo, tri, preferred_element_type=F32)


def _tri_strict(n):
    j = np.arange(n)
    return jnp.asarray((j[:, None] > j[None, :]).astype(np.float32), dtype=BF16)


def _sb_prompt_kernel(q_ref, k_ref, v_ref, tri_ref, o_ref, kb_sc, vb_sc, *, nq, scale):
    tq, tk = SB_TQ, SB_TK
    qi = pl.program_id(2)

    @pl.when(qi == 0)
    def _():
        kb_sc[...] = k_ref[...].astype(BF16)
        vb_sc[...] = v_ref[...].astype(BF16)

    qb = q_ref[...].astype(BF16)
    tri = tri_ref[...]

    def attend(c):
        q0 = c * tq
        n_tiles = (q0 + tq) // tk
        tiles = []
        for n in range(n_tiles):
            start = n * tk
            z = lax.dot_general(qb, kb_sc[start:start + tk, :], NT_DIMS, preferred_element_type=F32) * scale
            past = None
            if start + tk > q0:
                row = lax.broadcasted_iota(jnp.int32, (tq, tk), 0) + q0
                col = lax.broadcasted_iota(jnp.int32, (tq, tk), 1) + start
                past = col < row
            log_beta, log_keep = _sb_logs(z, past)
            after = _sum_after(log_keep, tri)
            tiles.append((log_beta, after, after[:, 0:1] + log_keep[:, 0:1], past))
        carry = jnp.zeros((tq, 1), F32)
        acc = jnp.zeros((tq, HEAD_DIM), F32)
        for n in reversed(range(n_tiles)):
            log_beta, after, total, past = tiles[n]
            a = jnp.exp(log_beta + (after + carry))
            if past is not None:
                a = jnp.where(past, a, 0.0)
            acc = acc + jnp.dot(a.astype(BF16), vb_sc[n * tk:(n + 1) * tk, :], preferred_element_type=F32)
            carry = carry + total
        o_ref[...] = acc.astype(o_ref.dtype)

    for c in range(nq):
        pl.when(qi == c)(functools.partial(attend, c))


def _sb_prompt(q, k, v, batch, seq):
    tq, tk = SB_TQ, SB_TK
    n_heads = q.shape[1] // HEAD_DIM
    assert seq % tq == 0 and tq % tk == 0
    nq = seq // tq
    vmem = 2 * (2 * seq * HEAD_DIM * 4) + 2 * seq * HEAD_DIM * 2 + 6 * tq * seq * 4
    tile = lambda b, h, i: (b * nq + i, h)
    whole = lambda b, h, i: (b, h)
    return pl.pallas_call(
        functools.partial(_sb_prompt_kernel, nq=nq, scale=HEAD_DIM ** -0.5),
        grid=(batch, n_heads, nq),
        in_specs=[pl.BlockSpec((tq, HEAD_DIM), tile),
                  pl.BlockSpec((seq, HEAD_DIM), whole),
                  pl.BlockSpec((seq, HEAD_DIM), whole),
                  pl.BlockSpec((tk, tk), lambda b, h, i: (0, 0))],
        out_specs=pl.BlockSpec((tq, HEAD_DIM), tile),
        out_shape=jax.ShapeDtypeStruct((batch * seq, n_heads * HEAD_DIM), BF16),
        scratch_shapes=[pltpu.VMEM((seq, HEAD_DIM), BF16), pltpu.VMEM((seq, HEAD_DIM), BF16)],
        compiler_params=_params(("arbitrary", "arbitrary", "arbitrary"), vmem),
        name="sb_prompt",
    )(q, k, v, _tri_strict(tk))


def _first_rows(rows, n):
    s, r, w = rows.shape
    return jnp.zeros((s, n, w), rows.dtype).at[:, :r, :].set(rows)


def _head_match(shape, n_heads, row_head):
    assert n_heads & (n_heads - 1) == 0
    col = lax.broadcasted_iota(jnp.int32, shape, 1)
    return (col & (n_heads - 1)) == row_head


RING = 3


def _page_ring(t, n_total, row_starts, srcs, bufs, sems):
    def copies(u):
        slot = u % RING
        return [pltpu.make_async_copy(src.at[pl.ds(start, buf.shape[1]), :], buf.at[slot], sems.at[k, slot])
                for k, (src, buf, start) in enumerate(zip(srcs, bufs, row_starts(u)))]

    def start(u):
        for k, c in enumerate(copies(u)):
            c.start(priority=k % 2)

    @pl.when(t == 0)
    def _():
        for u in range(min(RING - 1, n_total)):
            start(u)

    @pl.when(t + (RING - 1) < n_total)
    def _():
        start(t + (RING - 1))

    for c in copies(t):
        c.wait()
    return [buf.at[t % RING] for buf in bufs]


def _diff_sample_kernel(pt_ref, q_ref, kc_hbm, vc_hbm, kn_ref, vn_ref, brow_ref, lq1, lk1, lq2, lk2, g_ref,
                        o_ref, m_sc, l_sc, acc_sc, kbuf, vbuf, sems, *, n_seq, n_pages, n_heads, lam_init, scale):
    i = pl.program_id(0)
    p = pl.program_id(1)
    rows_kv = PAGE_SIZE * n_heads

    @pl.when(p == 0)
    def _():
        _softmax_init(m_sc, l_sc, acc_sc)

    def row_starts(u):
        page = pt_ref[u // n_pages, u % n_pages]
        return (pl.multiple_of(page * (2 * rows_kv), 2 * rows_kv), pl.multiple_of(page * rows_kv, rows_kv))

    def process(k_ref, v_ref, kind):
        s = jnp.concatenate(
            [lax.dot_general(q_ref[c], k_ref[pl.ds(c, rows_kv, stride=2), :].astype(BF16), NT_DIMS,
                             preferred_element_type=F32) for c in range(2)], axis=0)
        s = s * scale + brow_ref[kind]
        row = lax.broadcasted_iota(jnp.int32, s.shape, 0)
        s = jnp.where(_head_match(s.shape, n_heads, row & (n_heads - 1)), s, NEG)
        _softmax_step(s, v_ref[...].astype(BF16), m_sc, l_sc, acc_sc)

    @pl.when(p < n_pages)
    def _():
        k_view, v_view = _page_ring(i * n_pages + p, n_seq * n_pages, row_starts, (kc_hbm, vc_hbm),
                                    (kbuf, vbuf), sems)
        process(k_view, v_view, jnp.where(p == n_pages - 1, 1, 0))

    @pl.when(p == n_pages)
    def _():
        process(kn_ref, vn_ref, 2)
        lam = _diff_lambda(lq1, lk1, lq2, lk2, lam_init)
        o0 = acc_sc[0:n_heads, :] / l_sc[0:n_heads, :]
        o1 = acc_sc[n_heads:2 * n_heads, :] / l_sc[n_heads:2 * n_heads, :]
        o_ref[...] = _sub_rmsnorm(o0 - lam * o1, g_ref, lam_init)


def _diff_sample(q, k_new, v_new, cache_k, cache_v, page_table, brow, lams, sub_g, lam_init):
    s, n_heads, _, dh = q.shape
    n_pages = page_table.shape[1]
    rk, rv = PAGE_SIZE * n_heads * 2, PAGE_SIZE * n_heads
    kern = functools.partial(_diff_sample_kernel, n_seq=s, n_pages=n_pages, n_heads=n_heads, lam_init=lam_init,
                             scale=HEAD_DIM ** -0.5)
    mine = lambda i, p, pt: (i, 0, 0)
    vec = pl.BlockSpec((1, HEAD_DIM), lambda i, p, pt: (0, 0))
    vmem = (2 * RING + 4) * rk * dh * 4 + 8 * rk * dh * 4
    return pl.pallas_call(
        kern,
        grid_spec=pltpu.PrefetchScalarGridSpec(
            num_scalar_prefetch=1, grid=(s, n_pages + 1),
            in_specs=[pl.BlockSpec((None, 2, n_heads, dh), lambda i, p, pt: (i, 0, 0, 0)),
                      pl.BlockSpec(memory_space=pltpu.HBM),
                      pl.BlockSpec(memory_space=pltpu.HBM),
                      pl.BlockSpec((None, rk, dh), mine),
                      pl.BlockSpec((None, rv, 2 * dh), mine),
                      pl.BlockSpec((3, 2 * n_heads, rv), lambda i, p, pt: (0, 0, 0)),
                      vec, vec, vec, vec,
                      pl.BlockSpec((1, 2 * dh), lambda i, p, pt: (0, 0))],
            out_specs=pl.BlockSpec((None, n_heads, 2 * dh), mine),
            scratch_shapes=[pltpu.VMEM((2 * n_heads, 1), F32), pltpu.VMEM((2 * n_heads, 1), F32),
                            pltpu.VMEM((2 * n_heads, 2 * dh), F32),
                            pltpu.VMEM((RING, rk, dh), F32), pltpu.VMEM((RING, rv, 2 * dh), F32),
                            pltpu.SemaphoreType.DMA((2, RING))]),
        out_shape=jax.ShapeDtypeStruct((s, n_heads, 2 * dh), F32),
        compiler_params=_params(("arbitrary", "arbitrary"), vmem),
        name="diff_sample",
    )(page_table, q.transpose(0, 2, 1, 3).astype(BF16), cache_k, cache_v,
      _first_rows(k_new, rk), _first_rows(v_new, rv), brow, *lams, sub_g.reshape(1, 2 * dh))


def _sb_sample_kernel(pt_ref, q_ref, kc_hbm, vc_hbm, tri_ref, o_ref, c_sc, acc_sc, kbuf, vbuf, sems,
                      *, n_seq, n_pages, n_heads, scale):
    i = pl.program_id(0)
    p = pl.program_id(1)
    rows = PAGE_SIZE * n_heads
    n_chunks = rows // LANES

    @pl.when(p == 0)
    def _():
        c_sc[...] = jnp.zeros_like(c_sc)
        acc_sc[...] = jnp.zeros_like(acc_sc)

    def row_starts(u):
        start = pl.multiple_of(pt_ref[u // n_pages, n_pages - 1 - u % n_pages] * rows, rows)
        return (start, start)

    kc_ref, vc_ref = _page_ring(i * n_pages + p, n_seq * n_pages, row_starts, (kc_hbm, vc_hbm), (kbuf, vbuf), sems)
    z = lax.dot_general(q_ref[...], kc_ref[...].astype(BF16), NT_DIMS, preferred_element_type=F32) * scale
    z3 = jnp.stack([z[:, j * LANES:(j + 1) * LANES] for j in range(n_chunks)])
    row = lax.broadcasted_iota(jnp.int32, (n_heads, LANES), 0)
    valid = _head_match((n_heads, LANES), n_heads, row)[None]
    log_beta, log_keep = _sb_logs(z3, valid)
    after = _sum_after(log_keep.reshape(n_chunks * n_heads, LANES), tri_ref[...]).reshape(n_chunks, n_heads, LANES)
    total = after[:, :, 0:1] + log_keep[:, :, 0:1]
    run = c_sc[...]
    later = [None] * n_chunks
    for j in reversed(range(n_chunks)):
        later[j] = run
        run = run + total[j]
    c_sc[...] = run
    a3 = jnp.where(valid, jnp.exp(log_beta + (after + jnp.stack(later))), 0.0)
    a = jnp.concatenate([a3[j] for j in range(n_chunks)], axis=1).astype(BF16)
    acc_sc[...] += jnp.dot(a, vc_ref[...].astype(BF16), preferred_element_type=F32)

    @pl.when(p == n_pages - 1)
    def _():
        o_ref[...] = acc_sc[...]


def _sb_sample(q, cache_k, cache_v, page_table):
    s, n_heads, dh = q.shape
    n_pages = page_table.shape[1]
    rows = PAGE_SIZE * n_heads
    assert LANES % n_heads == 0
    kern = functools.partial(_sb_sample_kernel, n_seq=s, n_pages=n_pages, n_heads=n_heads, scale=HEAD_DIM ** -0.5)
    mine = lambda i, p, pt: (i, 0, 0)
    vmem = 2 * RING * rows * dh * 4 + 4 * rows * dh * 2 + 16 * n_heads * rows * 4
    return pl.pallas_call(
        kern,
        grid_spec=pltpu.PrefetchScalarGridSpec(
            num_scalar_prefetch=1, grid=(s, n_pages),
            in_specs=[pl.BlockSpec((None, n_heads, dh), mine),
                      pl.BlockSpec(memory_space=pltpu.HBM),
                      pl.BlockSpec(memory_space=pltpu.HBM),
                      pl.BlockSpec((LANES, LANES), lambda i, p, pt: (0, 0))],
            out_specs=pl.BlockSpec((None, n_heads, dh), mine),
            scratch_shapes=[pltpu.VMEM((n_heads, 1), F32), pltpu.VMEM((n_heads, dh), F32),
                            pltpu.VMEM((RING, rows, dh), F32), pltpu.VMEM((RING, rows, dh), F32),
                            pltpu.SemaphoreType.DMA((2, RING))]),
        out_shape=jax.ShapeDtypeStruct((s, n_heads, dh), F32),
        compiler_params=_params(("arbitrary", "arbitrary"), vmem),
        name="sb_sample",
    )(page_table, q.astype(BF16), cache_k, cache_v, _tri_strict(LANES))


def _moba_kmean_kernel(pt_ref, k_hbm, o_ref, abuf, bbuf, sems, *, n_seq, n_blocks, n_heads):
    rows = PAGE_SIZE * n_heads
    n = pl.program_id(1)

    def row_starts(u):
        i, blk = u // n_blocks, u % n_blocks
        return (pl.multiple_of(pt_ref[i, 2 * blk] * rows, rows), pl.multiple_of(pt_ref[i, 2 * blk + 1] * rows, rows))

    ka_ref, kb_ref = _page_ring(pl.program_id(0) * n_blocks + n, n_seq * n_blocks, row_starts, (k_hbm, k_hbm),
                                (abuf, bbuf), sems)
    shape = (PAGE_SIZE, n_heads, HEAD_DIM)
    tot = jnp.sum(ka_ref[...].reshape(shape), axis=0) + jnp.sum(kb_ref[...].reshape(shape), axis=0)
    o_ref[...] = tot * (1.0 / MOBA_BLOCK)


def _moba_kmean(cache_k, page_table, n_heads):
    assert MOBA_BLOCK == 2 * PAGE_SIZE and n_heads % 8 == 0
    s, n_pages = page_table.shape
    assert n_pages % 2 == 0
    nblk = n_pages // 2
    rows = PAGE_SIZE * n_heads
    return pl.pallas_call(
        functools.partial(_moba_kmean_kernel, n_seq=s, n_blocks=nblk, n_heads=n_heads),
        grid_spec=pltpu.PrefetchScalarGridSpec(
            num_scalar_prefetch=1, grid=(s, nblk),
            in_specs=[pl.BlockSpec(memory_space=pltpu.HBM)],
            out_specs=pl.BlockSpec((None, None, n_heads, HEAD_DIM), lambda i, n, pt: (i, n, 0, 0)),
            scratch_shapes=[pltpu.VMEM((RING, rows, HEAD_DIM), F32), pltpu.VMEM((RING, rows, HEAD_DIM), F32),
                            pltpu.SemaphoreType.DMA((2, RING))]),
        out_shape=jax.ShapeDtypeStruct((s, nblk, n_heads, HEAD_DIM), F32),
        compiler_params=_params(("arbitrary", "arbitrary"), (2 * RING + 4) * rows * HEAD_DIM * 4),
        name="moba_kmean",
    )(page_table, cache_k)


def _moba_top_kernel(km_ref, q_ref, o_ref):
    nblk = km_ref.shape[0]
    gates = jnp.sum(km_ref[...] * q_ref[...][None], axis=2, keepdims=True)
    blk = lax.broadcasted_iota(jnp.int32, gates.shape, 0).astype(F32)
    for r in range(MOBA_TOPK):
        best = jnp.max(gates, axis=0, keepdims=True)
        idx = jnp.min(jnp.where(gates == best, blk, float(nblk)), axis=0, keepdims=True)
        o_ref[r] = idx[0].astype(jnp.int32)
        gates = jnp.where(blk == idx, -jnp.inf, gates)


def _moba_top(kmean, q):
    s, nblk, n_heads, dh = kmean.shape
    assert nblk >= MOBA_TOPK
    return pl.pallas_call(
        _moba_top_kernel,
        grid=(s,),
        in_specs=[pl.BlockSpec((None, nblk, n_heads, dh), lambda i: (i, 0, 0, 0)),
                  pl.BlockSpec((None, n_heads, dh), lambda i: (i, 0, 0))],
        out_specs=pl.BlockSpec((None, MOBA_TOPK, n_heads, 1), lambda i: (i, 0, 0, 0)),
        out_shape=jax.ShapeDtypeStruct((s, MOBA_TOPK, n_heads, 1), jnp.int32),
        compiler_params=_params(("arbitrary",), 6 * nblk * n_heads * dh * 4),
        name="moba_top",
    )(kmean, q)


def _moba_attend_kernel(phys_ref, logi_ref, q_ref, kc_hbm, vc_hbm, kn_ref, vn_ref, brow_ref, o_ref,
                        m_sc, l_sc, acc_sc, kbuf, vbuf, sems, *, n_seq, n_sel, n_pages, n_heads, scale):
    h = pl.program_id(1)
    j = pl.program_id(2)
    rows = PAGE_SIZE * n_heads
    t = (pl.program_id(0) * n_heads + h) * n_sel + j

    @pl.when(j == 0)
    def _():
        _softmax_init(m_sc, l_sc, acc_sc)

    def process(k_ref, v_ref, kind):
        s = lax.dot_general(q_ref[...], k_ref[...].astype(BF16), NT_DIMS, preferred_element_type=F32)
        s = s * scale + brow_ref[pl.ds(kind, 1), :]
        s = jnp.where(_head_match(s.shape, n_heads, h), s, NEG)
        _softmax_step(s, v_ref[...].astype(BF16), m_sc, l_sc, acc_sc)

    def row_starts(u):
        start = pl.multiple_of(phys_ref[u] * rows, rows)
        return (start, start)

    @pl.when(j < n_sel)
    def _():
        k_view, v_view = _page_ring(t, n_seq * n_heads * n_sel, row_starts, (kc_hbm, vc_hbm), (kbuf, vbuf), sems)
        process(k_view, v_view, jnp.where(logi_ref[t] == n_pages - 1, 1, 0))

    @pl.when(j == n_sel)
    def _():
        process(kn_ref, vn_ref, 2)
        o_ref[...] = acc_sc[...] / l_sc[...]


def _moba_attend(q, k_new, v_new, cache_k, cache_v, phys, logical, brow, n_pages):
    s, n_heads, dh = q.shape
    n_sel = phys.shape[2]
    rows = PAGE_SIZE * n_heads
    q8 = jnp.zeros((s, n_heads, 8, dh), BF16).at[:, :, 0, :].set(q.astype(BF16))
    kern = functools.partial(_moba_attend_kernel, n_seq=s, n_sel=n_sel, n_pages=n_pages, n_heads=n_heads,
                             scale=HEAD_DIM ** -0.5)
    mine = lambda i, h, j, ph, lg: (i, 0, 0)
    out = pl.pallas_call(
        kern,
        grid_spec=pltpu.PrefetchScalarGridSpec(
            num_scalar_prefetch=2, grid=(s, n_heads, n_sel + 1),
            in_specs=[pl.BlockSpec((None, None, 8, dh), lambda i, h, j, ph, lg: (i, h, 0, 0)),
                      pl.BlockSpec(memory_space=pltpu.HBM),
                      pl.BlockSpec(memory_space=pltpu.HBM),
                      pl.BlockSpec((None, rows, dh), mine),
                      pl.BlockSpec((None, rows, dh), mine),
                      pl.BlockSpec((None, 8, rows), lambda i, h, j, ph, lg: (h, 0, 0))],
            out_specs=pl.BlockSpec((None, None, 8, dh), lambda i, h, j, ph, lg: (i, h, 0, 0)),
            scratch_shapes=[pltpu.VMEM((8, 1), F32), pltpu.VMEM((8, 1), F32), pltpu.VMEM((8, dh), F32),
                            pltpu.VMEM((RING, rows, dh), F32), pltpu.VMEM((RING, rows, dh), F32),
                            pltpu.SemaphoreType.DMA((2, RING))]),
        out_shape=jax.ShapeDtypeStruct((s, n_heads, 8, dh), F32),
        compiler_params=_params(("arbitrary", "arbitrary", "arbitrary"), (2 * RING + 10) * rows * dh * 4),
        name="moba_attend",
    )(phys.reshape(-1), logical.reshape(-1), q8, cache_k, cache_v, _first_rows(k_new, rows),
      _first_rows(v_new, rows), brow)
    return out[:, :, 0, :]


def _moba_sample(q, k_new, v_new, cache_k, cache_v, page_table, brow):
    s, n_heads, dh = q.shape
    n_pages = page_table.shape[1]
    kmean = _moba_kmean(cache_k, page_table, n_heads)
    blocks = _moba_top(kmean, q)[..., 0].transpose(0, 2, 1)
    logical = (2 * blocks[..., None] + jnp.arange(2, dtype=jnp.int32)).reshape(s, n_heads, 2 * MOBA_TOPK)
    phys = jnp.take_along_axis(page_table[:, None, :], logical, axis=2)
    return _moba_attend(q, k_new, v_new, cache_k, cache_v, phys, logical, brow, n_pages)


MM_PROMPT = dict(tm=1024, tn=512, tk=4096)
MM_PROMPT_DOWN = dict(tm=1024, tn=512, tk=5504)
MM_SAMPLE = dict(tm=16, tn=512, tk=16384)


def _lam_init(layer):
    return 0.8 - 0.6 * math.exp(-0.3 * layer)


def _ffn_and_ple(h, p_rows, up_gate, i, wts, mm, mm_down):
    f = _rmsnorm(h, wts['g_ffn'][i], BF16)
    act, g_rows = up_gate(f, wts['w_up'])
    h = _mm(act, wts['w_down'][i], extras=(h,), epilogue=_add_residual, name="down", **mm_down)
    pp = _mm(p_rows[i], wts['w_ple_proj'][i], name="ple_proj", **mm)
    a = _rmsnorm(h, wts['g_ple'][i], BF16)
    h = _mm(a, wts['w_ple_gate'][i], extras=(h, pp), epilogue=_ple_combine, name="ple_gate", **mm)
    return h, g_rows


def kernel(x_prompt, x_sample, cache_moba_k, cache_moba_v, cache_diff_k, cache_diff_v, cache_sb_k, cache_sb_v,
           state_conv, page_table, p_prompt, p_sample, rel_bias, w_in_even, w_out_even, lam_q1, lam_k1, lam_q2,
           lam_k2, diff_subln, w_in_odd, w_out_odd, g_mix, g_ffn, w_up, conv_w, conv_b, w_down, g_ple,
           w_ple_gate, w_ple_proj, g_final):
    batch, seq, d = x_prompt.shape
    dec = x_sample.shape[0]
    depth = g_mix.shape[0]
    assert depth == 2 and x_sample.shape[1] == 1
    dh = HEAD_DIM
    moba_heads = cache_moba_k.shape[3]
    diff_heads = cache_diff_k.shape[3]
    sb_heads = cache_sb_k.shape[3]
    moba_w, diff_w, sb_w = moba_heads * dh, diff_heads * 2 * dh, sb_heads * dh
    dff = conv_w.shape[2]
    n_pages = page_table.shape[1]
    m_p = batch * seq
    m_s = BF16_SUBLANES

    per_layer = lambda w: [w[i].astype(BF16) for i in range(depth)]
    wts = dict(g_ffn=g_ffn, g_ple=g_ple, w_up=w_up, w_down=per_layer(w_down),
               w_ple_gate=per_layer(w_ple_gate), w_ple_proj=per_layer(w_ple_proj))
    cuts_e = np.cumsum([0, moba_w, moba_w, moba_w, diff_w, diff_w, diff_w])
    w_in_e = [w_in_even[0][:, lo:hi].astype(BF16) for lo, hi in zip(cuts_e[:-1], cuts_e[1:])]
    w_in_o = [w_in_odd[0][:, j * sb_w:(j + 1) * sb_w].astype(BF16) for j in range(3)]
    w_out_e_m = w_out_even[0][:moba_w].astype(BF16)
    w_out_e_d = w_out_even[0][moba_w:].astype(BF16)
    w_out_o = w_out_odd[0].astype(BF16)
    lams = tuple(v[0].reshape(1, dh) for v in (lam_q1, lam_k1, lam_q2, lam_k2))
    sub_g = diff_subln[0]
    lam0 = _lam_init(0)

    rb_flat = rel_bias.T.reshape(-1)
    bias_tab = _prompt_bias_tables(rb_flat)
    bias_rows = _sample_bias_rows(rb_flat)

    h = x_prompt.reshape(m_p, d)
    pp_rows = p_prompt.reshape(depth, m_p, -1).astype(BF16)
    tail_tm = 1024
    up_gate_p = lambda i: (lambda f, w: _up_gate_prompt(f, w, i, conv_w[i], conv_b[i], seq, tm=tail_tm))

    a = _rmsnorm(h, g_mix[0], BF16)
    mq, mk, mv, dq, dk, dv = [_mm(a, w, name="in_proj", **MM_PROMPT) for w in w_in_e]
    mo = _moba_prompt(mq, mk, mv, bias_tab, rb_flat, batch, seq)
    do = _diff_prompt(dq, dk, dv, bias_tab, rb_flat, lams, sub_g, batch, seq, head0=moba_heads, lam_init=lam0)
    h = _mm(mo, w_out_e_m, extras=(h,), epilogue=_add_residual, name="out_proj", **MM_PROMPT)
    h = _mm(do, w_out_e_d, extras=(h,), epilogue=_add_residual, name="out_proj", **MM_PROMPT)
    h, tail0 = _ffn_and_ple(h, pp_rows, up_gate_p(0), 0, wts, MM_PROMPT, MM_PROMPT_DOWN)

    a = _rmsnorm(h, g_mix[1], BF16)
    sq, sk, sv = [_mm(a, w, name="in_proj", **MM_PROMPT) for w in w_in_o]
    so = _sb_prompt(sq, sk, sv, batch, seq)
    h = _mm(so, w_out_o, extras=(h,), epilogue=_add_residual, name="out_proj", **MM_PROMPT)
    h, tail1 = _ffn_and_ple(h, pp_rows, up_gate_p(1), 1, wts, MM_PROMPT, MM_PROMPT_DOWN)
    y_prompt = _rmsnorm(h, g_final, F32).reshape(batch, seq, d)

    def conv_rows(tail):
        per_seq = tail.reshape(batch, seq // tail_tm, 8, dff)
        return per_seq[:, -1, 8 - (CONV_W - 1):, :]

    conv_p = jnp.stack([conv_rows(tail0), conv_rows(tail1)])
    rows5 = lambda x, *shape: x.reshape((1, batch, seq) + shape)

    pad = lambda r: jnp.zeros((m_s,) + r.shape[1:], r.dtype).at[:dec].set(r)
    hs = pad(x_sample.reshape(dec, d))
    ps_rows = jnp.stack([pad(p_sample[i].reshape(dec, -1)) for i in range(depth)]).astype(BF16)

    def up_gate_s(i):
        s0, s1 = pad(state_conv[i, :, 0]), pad(state_conv[i, :, 1])
        return lambda f, w: _up_gate_sample(f, w, i, s0, s1, conv_w[i], conv_b[i])

    a = _rmsnorm(hs, g_mix[0], BF16)
    mq_s, mk_s, mv_s, dq_s, dk_s, dv_s = [_mm(a, w, name="in_proj_s", **MM_SAMPLE)[:dec] for w in w_in_e]
    brow_m = jnp.repeat(bias_rows[:moba_heads], moba_heads, axis=2)
    per_map = bias_rows[moba_heads:, :3].transpose(1, 0, 2)
    brow_d = jnp.repeat(jnp.concatenate([per_map, per_map], axis=1), diff_heads, axis=2)
    heads3 = lambda x, n: x.reshape(dec, n, -1)
    mo_s = _moba_sample(heads3(mq_s, moba_heads), heads3(mk_s, moba_heads), heads3(mv_s, moba_heads),
                        cache_moba_k[0].reshape(-1, dh), cache_moba_v[0].reshape(-1, dh), page_table, brow_m)
    do_s = _diff_sample(dq_s.reshape(dec, diff_heads, 2, dh), heads3(dk_s, 2 * diff_heads), heads3(dv_s, diff_heads),
                        cache_diff_k[0].reshape(-1, dh), cache_diff_v[0].reshape(-1, 2 * dh), page_table, brow_d,
                        lams, sub_g, lam0)
    mo_s, do_s = mo_s.reshape(dec, moba_w), do_s.reshape(dec, diff_w)
    hs = _mm(pad(mo_s).astype(BF16), w_out_e_m, extras=(hs,), epilogue=_add_residual, name="out_proj_s", **MM_SAMPLE)
    hs = _mm(pad(do_s).astype(BF16), w_out_e_d, extras=(hs,), epilogue=_add_residual, name="out_proj_s", **MM_SAMPLE)
    hs, gs0 = _ffn_and_ple(hs, ps_rows, up_gate_s(0), 0, wts, MM_SAMPLE, MM_SAMPLE)

    a = _rmsnorm(hs, g_mix[1], BF16)
    sq_s, sk_s, sv_s = [_mm(a, w, name="in_proj_s", **MM_SAMPLE)[:dec] for w in w_in_o]
    so_s = _sb_sample(sq_s.reshape(dec, sb_heads, dh), cache_sb_k[0].reshape(-1, dh),
                      cache_sb_v[0].reshape(-1, dh), page_table)
    hs = _mm(pad(so_s.reshape(dec, sb_w)).astype(BF16), w_out_o, extras=(hs,), epilogue=_add_residual,
             name="out_proj_s", **MM_SAMPLE)
    hs, gs1 = _ffn_and_ple(hs, ps_rows, up_gate_s(1), 1, wts, MM_SAMPLE, MM_SAMPLE)
    y_sample = _rmsnorm(hs, g_final, F32)[:dec].reshape(dec, 1, d)

    conv_s = jnp.stack([jnp.stack([state_conv[i, :, 1], g[:dec]], axis=1) for i, g in enumerate((gs0, gs1))])

    return (y_prompt, y_sample,
            rows5(mk, moba_heads, dh), rows5(mv, moba_heads, dh),
            rows5(dk, diff_heads, 2, dh), rows5(dv, diff_heads, 2 * dh),
            rows5(sk, sb_heads, dh), rows5(sv, sb_heads, dh), conv_p,
            mk_s.reshape(1, dec, 1, moba_heads, dh), mv_s.reshape(1, dec, 1, moba_heads, dh),
            dk_s.reshape(1, dec, 1, diff_heads, 2, dh), dv_s.reshape(1, dec, 1, diff_heads, 2 * dh),
            sk_s.reshape(1, dec, 1, sb_heads, dh), sv_s.reshape(1, dec, 1, sb_heads, dh), conv_s)
```

```python
import functools
import math

import numpy as np
import jax
import jax.numpy as jnp
from jax import lax
from jax.experimental import pallas as pl
from jax.experimental.pallas import tpu as pltpu

F32 = jnp.float32
BF16 = jnp.bfloat16

HEAD_DIM = 128
MOBA_BLOCK = 256
MOBA_TOPK = 3
NUM_BUCKETS = 32
MAX_DISTANCE = 128
PAGE_SIZE = 128
CONV_W = 3
EPS = 1e-6
NEG = -1e30
ATT_TILE = 256
SB_TQ = 512
SB_TK = 256
UP_ROWS = 256
LANES = 128
BF16_SUBLANES = 16
V7X_VMEM_BYTES = 64 * 1024 * 1024
VMEM_CAP = V7X_VMEM_BYTES - 8 * 1024 * 1024
NT_DIMS = (((1,), (1,)), ((), ()))


def _params(sem, vmem_bytes):
    limit = int(min(max(vmem_bytes * 5 // 4, 16 * 1024 * 1024), VMEM_CAP))
    return pltpu.CompilerParams(dimension_semantics=sem, vmem_limit_bytes=limit)


def _bucket_np(dist):
    n = np.maximum(dist, 0)
    max_exact = NUM_BUCKETS // 2
    nf = np.maximum(n, 1).astype(np.float32)
    large = max_exact + (np.log(nf / np.float32(max_exact)) / np.float32(math.log(MAX_DISTANCE / max_exact))
                         * np.float32(NUM_BUCKETS - max_exact)).astype(np.int32)
    return np.where(n < max_exact, n, np.minimum(large, NUM_BUCKETS - 1)).astype(np.int32)


def _bias_table_kernel(rb_ref, bkt_ref, mask_ref, o_ref):
    h = pl.program_id(0)
    bkt = bkt_ref[...]
    acc = jnp.zeros(bkt.shape, F32)
    for b in range(NUM_BUCKETS):
        acc = jnp.where(bkt == b, rb_ref[h * NUM_BUCKETS + b], acc)
    o_ref[...] = acc + mask_ref[...]


def _bias_table(rb_flat, bkt, mask):
    n_heads = rb_flat.shape[0] // NUM_BUCKETS
    r, c = bkt.shape
    return pl.pallas_call(
        _bias_table_kernel,
        grid_spec=pltpu.PrefetchScalarGridSpec(
            num_scalar_prefetch=1, grid=(n_heads,),
            in_specs=[pl.BlockSpec((r, c), lambda h, rb: (0, 0)),
                      pl.BlockSpec((r, c), lambda h, rb: (0, 0))],
            out_specs=pl.BlockSpec((None, r, c), lambda h, rb: (h, 0, 0))),
        out_shape=jax.ShapeDtypeStruct((n_heads, r, c), F32),
        compiler_params=_params(("arbitrary",), 8 * r * c * 4),
        name="bias_table",
    )(rb_flat, jnp.asarray(bkt), jnp.asarray(mask))


def _prompt_bias_tables(rb_flat):
    t = ATT_TILE
    r = np.arange(t)[:, None]
    c = np.arange(t)[None, :]
    d0 = r - c
    d1 = t + r - c
    assert int(_bucket_np(np.array([2 * t - (t - 1)]))[0]) == NUM_BUCKETS - 1
    bkt = np.concatenate([_bucket_np(d0), _bucket_np(d1)], axis=0)
    mask = np.concatenate([np.where(d0 >= 0, 0.0, NEG), np.zeros((t, t))], axis=0).astype(np.float32)
    return _bias_table(rb_flat, bkt, mask)


def _sample_bias_rows(rb_flat):
    assert int(_bucket_np(np.array([PAGE_SIZE + 1]))[0]) == NUM_BUCKETS - 1
    j = np.arange(PAGE_SIZE)
    bkt = np.zeros((8, PAGE_SIZE), np.int32)
    mask = np.zeros((8, PAGE_SIZE), np.float32)
    bkt[0] = NUM_BUCKETS - 1
    bkt[1] = _bucket_np(PAGE_SIZE - j)
    bkt[2] = 0
    mask[2, 1:] = NEG
    return _bias_table(rb_flat, bkt, mask)


def _rmsnorm_kernel(x_ref, g_ref, o_ref):
    x = x_ref[...]
    y = x * lax.rsqrt(jnp.mean(x * x, axis=-1, keepdims=True) + EPS)
    o_ref[...] = (y * g_ref[...]).astype(o_ref.dtype)


def _rmsnorm(x, g, out_dtype):
    m, d = x.shape
    tr = min(m, 256)
    assert m % tr == 0
    return pl.pallas_call(
        _rmsnorm_kernel,
        grid=(m // tr,),
        in_specs=[pl.BlockSpec((tr, d), lambda i: (i, 0)),
                  pl.BlockSpec((1, d), lambda i: (0, 0))],
        out_specs=pl.BlockSpec((tr, d), lambda i: (i, 0)),
        out_shape=jax.ShapeDtypeStruct((m, d), out_dtype),
        compiler_params=_params(("parallel",), 4 * tr * d * 4),
        name="rmsnorm",
    )(x, g.reshape(1, d))


def _mm_kernel(*refs, nk, n_extra, epilogue):
    a_ref, w_ref = refs[0], refs[1]
    extra = refs[2:2 + n_extra]
    o_ref = refs[2 + n_extra]

    def finish(acc):
        o_ref[...] = epilogue(acc, *[e[...] for e in extra]).astype(o_ref.dtype)

    if nk == 1:
        finish(jnp.dot(a_ref[...], w_ref[...], preferred_element_type=F32))
    else:
        acc_ref = refs[3 + n_extra]
        k = pl.program_id(2)

        @pl.when(k == 0)
        def _():
            acc_ref[...] = jnp.zeros_like(acc_ref)

        acc_ref[...] += jnp.dot(a_ref[...], w_ref[...], preferred_element_type=F32)

        @pl.when(k == nk - 1)
        def _():
            finish(acc_ref[...])


def _mm(a, w, *, tm, tn, tk, extras=(), epilogue=None, out_dtype=F32, name="matmul"):
    m, kdim = a.shape
    n = w.shape[1]
    tm, tn, tk = min(tm, m), min(tn, n), min(tk, kdim)
    assert m % tm == 0 and n % tn == 0 and kdim % tk == 0
    nk = kdim // tk
    if epilogue is None:
        epilogue = lambda acc: acc
    in_specs = [pl.BlockSpec((tm, tk), lambda i, j, k: (i, k)),
                pl.BlockSpec((tk, tn), lambda i, j, k: (k, j))]
    in_specs += [pl.BlockSpec((tm, tn), lambda i, j, k: (i, j)) for _ in extras]
    scratch = [pltpu.VMEM((tm, tn), F32)] if nk > 1 else []
    vmem = 2 * (tm * tk * 2 + tk * tn * 2) + (2 * len(extras) + 4) * tm * tn * 4
    return pl.pallas_call(
        functools.partial(_mm_kernel, nk=nk, n_extra=len(extras), epilogue=epilogue),
        grid=(m // tm, n // tn, nk),
        in_specs=in_specs,
        out_specs=pl.BlockSpec((tm, tn), lambda i, j, k: (i, j)),
        out_shape=jax.ShapeDtypeStruct((m, n), out_dtype),
        scratch_shapes=scratch,
        compiler_params=_params(("parallel", "parallel", "arbitrary"), vmem),
        name=name,
    )(a, w, *extras)


def _add_residual(acc, h):
    return h + acc


def _ple_combine(acc, h, pp):
    return h + jax.nn.sigmoid(acc) * pp


def _conv_taps(cw_ref, cb_ref, g2, g1, g0):
    gc = cb_ref[...] + cw_ref[0:1, :] * g2
    gc = gc + cw_ref[1:2, :] * g1
    return gc + cw_ref[2:3, :] * g0


def _up_gate_prompt_kernel(a_ref, ap_ref, wg_ref, wu_ref, cw_ref, cb_ref, act_ref, tail_ref, *, seq):
    tm = a_ref.shape[0]
    wg = wg_ref[...].astype(BF16)
    wu = wu_ref[...].astype(BF16)
    before = jnp.dot(ap_ref[...], wg, preferred_element_type=F32)[BF16_SUBLANES - 2:BF16_SUBLANES, :]
    row = lax.broadcasted_iota(jnp.int32, (UP_ROWS, 1), 0)
    for r0 in range(0, tm, UP_ROWS):
        a = a_ref[r0:r0 + UP_ROWS, :]
        g = jnp.dot(a, wg, preferred_element_type=F32)
        u = jnp.dot(a, wu, preferred_element_type=F32)
        pos = (pl.program_id(0) * tm + r0 + row) % seq
        g1 = jnp.where(row == 0, before[1:2, :], pltpu.roll(g, 1, axis=0))
        g1 = jnp.where(pos >= 1, g1, 0.0)
        g2 = jnp.where(row == 0, before[0:1, :], jnp.where(row == 1, before[1:2, :], pltpu.roll(g, 2, axis=0)))
        g2 = jnp.where(pos >= 2, g2, 0.0)
        gc = _conv_taps(cw_ref, cb_ref, g2, g1, g)
        act_ref[r0:r0 + UP_ROWS, :] = (jax.nn.silu(gc) * u).astype(act_ref.dtype)
        before = g[UP_ROWS - 2:UP_ROWS, :]
        if r0 + UP_ROWS == tm:
            tail_ref[...] = g[UP_ROWS - 8:UP_ROWS, :]


def _up_gate_prompt(a, w_up, layer, conv_w, conv_b, seq, tm=1024, tn=256):
    m, d = a.shape
    dff = conv_w.shape[1]
    assert m % tm == 0 and dff % tn == 0 and seq % tm == 0 and tm % UP_ROWS == 0
    nj = dff // tn
    per16 = tm // BF16_SUBLANES
    vmem = (2 * (tm * d * 2 + BF16_SUBLANES * d * 2 + 2 * d * tn * 4) + 2 * d * tn * 2
            + 2 * tm * tn * 2 + 12 * UP_ROWS * tn * 4)
    return pl.pallas_call(
        functools.partial(_up_gate_prompt_kernel, seq=seq),
        grid=(m // tm, nj),
        in_specs=[pl.BlockSpec((tm, d), lambda i, j: (i, 0)),
                  pl.BlockSpec((BF16_SUBLANES, d), lambda i, j: (jnp.maximum(i * per16 - 1, 0), 0)),
                  pl.BlockSpec((None, d, tn), lambda i, j: (layer, 0, j)),
                  pl.BlockSpec((None, d, tn), lambda i, j: (layer, 0, j + nj)),
                  pl.BlockSpec((CONV_W, tn), lambda i, j: (0, j)),
                  pl.BlockSpec((1, tn), lambda i, j: (0, j))],
        out_specs=[pl.BlockSpec((tm, tn), lambda i, j: (i, j)),
                   pl.BlockSpec((8, tn), lambda i, j: (i, j))],
        out_shape=[jax.ShapeDtypeStruct((m, dff), BF16),
                   jax.ShapeDtypeStruct((m // tm * 8, dff), F32)],
        compiler_params=_params(("parallel", "parallel"), vmem),
        name="up_gate_prompt",
    )(a, a, w_up, w_up, conv_w, conv_b.reshape(1, dff))


def _up_gate_sample_kernel(a_ref, wg_ref, wu_ref, s0_ref, s1_ref, cw_ref, cb_ref, act_ref, g_ref):
    a = a_ref[...]
    g = jnp.dot(a, wg_ref[...].astype(BF16), preferred_element_type=F32)
    u = jnp.dot(a, wu_ref[...].astype(BF16), preferred_element_type=F32)
    gc = _conv_taps(cw_ref, cb_ref, s0_ref[...], s1_ref[...], g)
    act_ref[...] = (jax.nn.silu(gc) * u).astype(act_ref.dtype)
    g_ref[...] = g


def _up_gate_sample(a, w_up, layer, s0, s1, conv_w, conv_b, tn=256):
    m, d = a.shape
    dff = conv_w.shape[1]
    assert dff % tn == 0
    nj = dff // tn
    col = lambda j: (0, j)
    return pl.pallas_call(
        _up_gate_sample_kernel,
        grid=(nj,),
        in_specs=[pl.BlockSpec((m, d), lambda j: (0, 0)),
                  pl.BlockSpec((None, d, tn), lambda j: (layer, 0, j)),
                  pl.BlockSpec((None, d, tn), lambda j: (layer, 0, j + nj)),
                  pl.BlockSpec((m, tn), col), pl.BlockSpec((m, tn), col),
                  pl.BlockSpec((CONV_W, tn), col), pl.BlockSpec((1, tn), col)],
        out_specs=[pl.BlockSpec((m, tn), col), pl.BlockSpec((m, tn), col)],
        out_shape=[jax.ShapeDtypeStruct((m, dff), BF16), jax.ShapeDtypeStruct((m, dff), F32)],
        compiler_params=_params(("parallel",), 2 * (m * d * 2 + 2 * d * tn * 4) + 2 * d * tn * 2 + 16 * m * tn * 4),
        name="up_gate_sample",
    )(a, w_up, w_up, s0, s1, conv_w, conv_b.reshape(1, dff))


def _softmax_step(s, v, m_ref, l_ref, acc_ref):
    m_prev = m_ref[...]
    m_new = jnp.maximum(m_prev, jnp.max(s, axis=1, keepdims=True))
    alpha = jnp.exp(m_prev - m_new)
    p = jnp.exp(s - m_new)
    l_ref[...] = alpha * l_ref[...] + jnp.sum(p, axis=1, keepdims=True)
    acc_ref[...] = alpha * acc_ref[...] + jnp.dot(p.astype(BF16), v, preferred_element_type=F32)
    m_ref[...] = m_new


def _softmax_tiles(scores, values):
    m = functools.reduce(jnp.maximum, [jnp.max(s, axis=1, keepdims=True) for s in scores])
    probs = [jnp.exp(s - m) for s in scores]
    l = functools.reduce(jnp.add, [jnp.sum(p, axis=1, keepdims=True) for p in probs])
    o = functools.reduce(jnp.add, [jnp.dot(p.astype(BF16), v, preferred_element_type=F32)
                                   for p, v in zip(probs, values)])
    return o, l


def _softmax_init(m_ref, l_ref, acc_ref):
    m_ref[...] = jnp.full_like(m_ref, NEG)
    l_ref[...] = jnp.zeros_like(l_ref)
    acc_ref[...] = jnp.zeros_like(acc_ref)


def _diff_lambda(lq1, lk1, lq2, lk2, lam_init):
    return (jnp.exp(jnp.sum(lq1[...] * lk1[...], axis=1, keepdims=True))
            - jnp.exp(jnp.sum(lq2[...] * lk2[...], axis=1, keepdims=True)) + lam_init)


def _sub_rmsnorm(o, g_ref, lam_init):
    return o * lax.rsqrt(jnp.mean(o * o, axis=-1, keepdims=True) + EPS) * (g_ref[...] * (1.0 - lam_init))


def _moba_prompt_kernel(rb_ref, q_ref, k_ref, v_ref, bias_ref, o_ref,
                        kb_sc, vb_sc, km_sc, *, nb, scale):
    blk = MOBA_BLOCK
    h = pl.program_id(1)
    qi = pl.program_id(2)

    @pl.when(qi == 0)
    def _():
        kb_sc[:, 0:HEAD_DIM] = k_ref[...].astype(BF16)
        row_blk = lax.broadcasted_iota(jnp.int32, (k_ref.shape[0], HEAD_DIM), 0) // blk
        lane = lax.broadcasted_iota(jnp.int32, (k_ref.shape[0], HEAD_DIM), 1)
        kb_sc[:, HEAD_DIM:2 * HEAD_DIM] = jnp.where(row_blk == lane, 1.0, 0.0).astype(BF16)
        vb_sc[...] = v_ref[...].astype(BF16)
        km_sc[...] = jnp.zeros_like(km_sc)
        for n in range(nb):
            km_sc[n:n + 1, :] = jnp.sum(k_ref[n * blk:(n + 1) * blk, :], axis=0, keepdims=True) * (1.0 / blk)

    q32 = q_ref[...]
    gate = lax.dot_general(km_sc[...], q32, NT_DIMS, precision=lax.Precision.HIGHEST,
                           preferred_element_type=F32)
    g8 = gate[0:8, :]
    blk_id = lax.broadcasted_iota(jnp.int32, g8.shape, 0)
    gm = jnp.where(blk_id < qi, g8, -jnp.inf)
    rank = jnp.zeros(g8.shape, jnp.int32)
    for m in range(nb - 1):
        other = gm[m:m + 1, :]
        ahead = (other > gm) | ((other == gm) & (m < blk_id))
        rank = rank + ahead.astype(jnp.int32)
    drop = jnp.where((blk_id < qi) & (rank >= MOBA_TOPK), NEG / scale, 0.0)
    drop = jnp.concatenate([drop, jnp.zeros((HEAD_DIM - 8, blk), F32)], axis=0).T
    qb = jnp.concatenate([q32.astype(BF16), drop.astype(BF16)], axis=1)

    far_bias = rb_ref[h * NUM_BUCKETS + NUM_BUCKETS - 1]

    def attend(c):
        scores, values = [], []
        for n in range(c + 1):
            rows = slice(n * blk, (n + 1) * blk)
            s = lax.dot_general(qb, kb_sc[rows, :], NT_DIMS, preferred_element_type=F32) * scale
            if n == c:
                s = s + bias_ref[0:blk, :]
            elif n == c - 1:
                s = s + bias_ref[blk:2 * blk, :]
            else:
                s = s + far_bias
            scores.append(s)
            values.append(vb_sc[rows, :])
        o, l = _softmax_tiles(scores, values)
        o_ref[...] = (o / l).astype(o_ref.dtype)

    for c in range(nb):
        pl.when(qi == c)(functools.partial(attend, c))


def _moba_prompt(q, k, v, bias_tab, rb_flat, batch, seq):
    blk = MOBA_BLOCK
    n_heads = q.shape[1] // HEAD_DIM
    assert seq % blk == 0 and blk == ATT_TILE
    nb = seq // blk
    assert nb <= 8
    kern = functools.partial(_moba_prompt_kernel, nb=nb, scale=HEAD_DIM ** -0.5)
    vmem = 2 * (2 * seq * HEAD_DIM * 4) + 2 * seq * HEAD_DIM * 2 + 2 * 2 * blk * blk * 4 + 4 * blk * seq * 4
    tile = lambda b, h, i, rb: (b * nb + i, h)
    whole = lambda b, h, i, rb: (b, h)
    return pl.pallas_call(
        kern,
        grid_spec=pltpu.PrefetchScalarGridSpec(
            num_scalar_prefetch=1, grid=(batch, n_heads, nb),
            in_specs=[pl.BlockSpec((blk, HEAD_DIM), tile),
                      pl.BlockSpec((seq, HEAD_DIM), whole),
                      pl.BlockSpec((seq, HEAD_DIM), whole),
                      pl.BlockSpec((None, 2 * blk, blk), lambda b, h, i, rb: (h, 0, 0))],
            out_specs=pl.BlockSpec((blk, HEAD_DIM), tile),
            scratch_shapes=[pltpu.VMEM((seq, 2 * HEAD_DIM), BF16), pltpu.VMEM((seq, HEAD_DIM), BF16),
                            pltpu.VMEM((HEAD_DIM, HEAD_DIM), F32)]),
        out_shape=jax.ShapeDtypeStruct((batch * seq, n_heads * HEAD_DIM), BF16),
        compiler_params=_params(("arbitrary", "arbitrary", "arbitrary"), vmem),
        name="moba_prompt",
    )(rb_flat, q, k, v, bias_tab)


def _diff_prompt_kernel(rb_ref, q_ref, k_ref, v_ref, bias_ref, lq1, lk1, lq2, lk2, g_ref, o_ref,
                        kb_sc, vb_sc, *, nq, head0, lam_init, scale):
    t = ATT_TILE
    dh = HEAD_DIM
    h = pl.program_id(1)
    qi = pl.program_id(2)

    @pl.when(qi == 0)
    def _():
        kb_sc[...] = k_ref[...].astype(BF16)
        vb_sc[...] = v_ref[...].astype(BF16)

    qb = q_ref[...].astype(BF16)
    far_bias = rb_ref[(head0 + h) * NUM_BUCKETS + NUM_BUCKETS - 1]

    def attend(c):
        values = [vb_sc[n * t:(n + 1) * t, :] for n in range(c + 1)]
        outs = []
        for mp in range(2):
            scores = []
            for n in range(c + 1):
                k = kb_sc[n * t:(n + 1) * t, mp * dh:(mp + 1) * dh]
                s = lax.dot_general(qb[:, mp * dh:(mp + 1) * dh], k, NT_DIMS, preferred_element_type=F32) * scale
                if n == c:
                    s = s + bias_ref[0:t, :]
                elif n == c - 1:
                    s = s + bias_ref[t:2 * t, :]
                else:
                    s = s + far_bias
                scores.append(s)
            o, l = _softmax_tiles(scores, values)
            outs.append(o / l)
        lam = _diff_lambda(lq1, lk1, lq2, lk2, lam_init)
        o_ref[...] = _sub_rmsnorm(outs[0] - lam * outs[1], g_ref, lam_init).astype(o_ref.dtype)

    for c in range(nq):
        pl.when(qi == c)(functools.partial(attend, c))


def _diff_prompt(q, k, v, bias_tab, rb_flat, lams, sub_g, batch, seq, head0, lam_init):
    t = ATT_TILE
    w = 2 * HEAD_DIM
    n_heads = q.shape[1] // w
    assert seq % t == 0
    nq = seq // t
    kern = functools.partial(_diff_prompt_kernel, nq=nq, head0=head0, lam_init=lam_init, scale=HEAD_DIM ** -0.5)
    vec = pl.BlockSpec((1, HEAD_DIM), lambda b, h, i, rb: (0, 0))
    vmem = 2 * (2 * seq * w * 4) + 2 * seq * w * 2 + 2 * 2 * t * t * 4 + 8 * t * seq * 4
    tile = lambda b, h, i, rb: (b * nq + i, h)
    whole = lambda b, h, i, rb: (b, h)
    return pl.pallas_call(
        kern,
        grid_spec=pltpu.PrefetchScalarGridSpec(
            num_scalar_prefetch=1, grid=(batch, n_heads, nq),
            in_specs=[pl.BlockSpec((t, w), tile),
                      pl.BlockSpec((seq, w), whole),
                      pl.BlockSpec((seq, w), whole),
                      pl.BlockSpec((None, 2 * t, t), lambda b, h, i, rb: (head0 + h, 0, 0)),
                      vec, vec, vec, vec,
                      pl.BlockSpec((1, w), lambda b, h, i, rb: (0, 0))],
            out_specs=pl.BlockSpec((t, w), tile),
            scratch_shapes=[pltpu.VMEM((seq, w), BF16), pltpu.VMEM((seq, w), BF16)]),
        out_shape=jax.ShapeDtypeStruct((batch * seq, n_heads * w), BF16),
        compiler_params=_params(("arbitrary", "arbitrary", "arbitrary"), vmem),
        name="diff_prompt",
    )(rb_flat, q, k, v, bias_tab, *lams, sub_g.reshape(1, w))


def _sb_logs(z, valid):
    log_beta = jnp.minimum(z, 0.0) - jnp.log(1.0 + jnp.exp(-jnp.abs(z)))
    log_keep = log_beta - z
    if valid is not None:
        log_keep = jnp.where(valid, log_keep, 0.0)
    return log_beta, log_keep


def _sum_after(log_keep, tri):
    hi = log_keep.astype(BF16)
    lo = (log_keep - hi.astype(F32)).astype(BF16)
    return jnp.dot(hi, tri, preferred_element_type=F32) + jnp.dot(lo, tri, preferred_element_type=F32)


def _tri_strict(n):
    j = np.arange(n)
    return jnp.asarray((j[:, None] > j[None, :]).astype(np.float32), dtype=BF16)


def _sb_prompt_kernel(q_ref, k_ref, v_ref, tri_ref, o_ref, kb_sc, vb_sc, *, nq, scale):
    tq, tk = SB_TQ, SB_TK
    qi = pl.program_id(2)

    @pl.when(qi == 0)
    def _():
        kb_sc[...] = k_ref[...].astype(BF16)
        vb_sc[...] = v_ref[...].astype(BF16)

    qb = q_ref[...].astype(BF16)
    tri = tri_ref[...]

    def attend(c):
        q0 = c * tq
        n_tiles = (q0 + tq) // tk
        tiles = []
        for n in range(n_tiles):
            start = n * tk
            z = lax.dot_general(qb, kb_sc[start:start + tk, :], NT_DIMS, preferred_element_type=F32) * scale
            past = None
            if start + tk > q0:
                row = lax.broadcasted_iota(jnp.int32, (tq, tk), 0) + q0
                col = lax.broadcasted_iota(jnp.int32, (tq, tk), 1) + start
                past = col < row
            log_beta, log_keep = _sb_logs(z, past)
            after = _sum_after(log_keep, tri)
            tiles.append((log_beta, after, after[:, 0:1] + log_keep[:, 0:1], past))
        carry = jnp.zeros((tq, 1), F32)
        acc = jnp.zeros((tq, HEAD_DIM), F32)
        for n in reversed(range(n_tiles)):
            log_beta, after, total, past = tiles[n]
            a = jnp.exp(log_beta + (after + carry))
            if past is not None:
                a = jnp.where(past, a, 0.0)
            acc = acc + jnp.dot(a.astype(BF16), vb_sc[n * tk:(n + 1) * tk, :], preferred_element_type=F32)
            carry = carry + total
        o_ref[...] = acc.astype(o_ref.dtype)

    for c in range(nq):
        pl.when(qi == c)(functools.partial(attend, c))


def _sb_prompt(q, k, v, batch, seq):
    tq, tk = SB_TQ, SB_TK
    n_heads = q.shape[1] // HEAD_DIM
    assert seq % tq == 0 and tq % tk == 0
    nq = seq // tq
    vmem = 2 * (2 * seq * HEAD_DIM * 4) + 2 * seq * HEAD_DIM * 2 + 6 * tq * seq * 4
    tile = lambda b, h, i: (b * nq + i, h)
    whole = lambda b, h, i: (b, h)
    return pl.pallas_call(
        functools.partial(_sb_prompt_kernel, nq=nq, scale=HEAD_DIM ** -0.5),
        grid=(batch, n_heads, nq),
        in_specs=[pl.BlockSpec((tq, HEAD_DIM), tile),
                  pl.BlockSpec((seq, HEAD_DIM), whole),
                  pl.BlockSpec((seq, HEAD_DIM), whole),
                  pl.BlockSpec((tk, tk), lambda b, h, i: (0, 0))],
        out_specs=pl.BlockSpec((tq, HEAD_DIM), tile),
        out_shape=jax.ShapeDtypeStruct((batch * seq, n_heads * HEAD_DIM), BF16),
        scratch_shapes=[pltpu.VMEM((seq, HEAD_DIM), BF16), pltpu.VMEM((seq, HEAD_DIM), BF16)],
        compiler_params=_params(("arbitrary", "arbitrary", "arbitrary"), vmem),
        name="sb_prompt",
    )(q, k, v, _tri_strict(tk))


def _first_rows(rows, n):
    s, r, w = rows.shape
    return jnp.zeros((s, n, w), rows.dtype).at[:, :r, :].set(rows)


def _head_match(shape, n_heads, row_head):
    assert n_heads & (n_heads - 1) == 0
    col = lax.broadcasted_iota(jnp.int32, shape, 1)
    return (col & (n_heads - 1)) == row_head


RING = 3
SB_PAGES = 2


def _page_ring(t, n_total, row_starts, srcs, bufs, sems):
    def copies(u):
        slot = u % RING
        return [pltpu.make_async_copy(src.at[pl.ds(start, buf.shape[1]), :], buf.at[slot], sems.at[k, slot])
                for k, (src, buf, start) in enumerate(zip(srcs, bufs, row_starts(u)))]

    def start(u):
        for k, c in enumerate(copies(u)):
            c.start(priority=k % 2)

    @pl.when(t == 0)
    def _():
        for u in range(min(RING - 1, n_total)):
            start(u)

    @pl.when(t + (RING - 1) < n_total)
    def _():
        start(t + (RING - 1))

    for c in copies(t):
        c.wait()
    return [buf.at[t % RING] for buf in bufs]


def _diff_sample_kernel(pt_ref, q_ref, kc_hbm, vc_hbm, kn_ref, vn_ref, brow_ref, lq1, lk1, lq2, lk2, g_ref,
                        o_ref, m_sc, l_sc, acc_sc, kbuf0, kbuf1, vbuf0, vbuf1, sems,
                        *, n_seq, n_pages, n_heads, lam_init, scale):
    i = pl.program_id(0)
    p = pl.program_id(1)
    rows_kv = PAGE_SIZE * n_heads
    n_steps = n_pages // 2

    @pl.when(p == 0)
    def _():
        _softmax_init(m_sc, l_sc, acc_sc)

    def row_starts(u):
        pages = [pt_ref[u // n_steps, 2 * (u % n_steps) + e] for e in range(2)]
        return (tuple(pl.multiple_of(pg * (2 * rows_kv), 2 * rows_kv) for pg in pages)
                + tuple(pl.multiple_of(pg * rows_kv, rows_kv) for pg in pages))

    def process(k_ref, v_ref, kind):
        s = jnp.concatenate(
            [lax.dot_general(q_ref[c], k_ref[pl.ds(c, rows_kv, stride=2), :].astype(BF16), NT_DIMS,
                             preferred_element_type=F32) for c in range(2)], axis=0)
        s = s * scale + brow_ref[kind]
        row = lax.broadcasted_iota(jnp.int32, s.shape, 0)
        s = jnp.where(_head_match(s.shape, n_heads, row & (n_heads - 1)), s, NEG)
        _softmax_step(s, v_ref[...].astype(BF16), m_sc, l_sc, acc_sc)

    @pl.when(p < n_steps)
    def _():
        views = _page_ring(i * n_steps + p, n_seq * n_steps, row_starts, (kc_hbm, kc_hbm, vc_hbm, vc_hbm),
                           (kbuf0, kbuf1, vbuf0, vbuf1), sems)
        process(views[0], views[2], 0)
        process(views[1], views[3], jnp.where(p == n_steps - 1, 1, 0))

    @pl.when(p == n_steps)
    def _():
        process(kn_ref, vn_ref, 2)
        lam = _diff_lambda(lq1, lk1, lq2, lk2, lam_init)
        o0 = acc_sc[0:n_heads, :] / l_sc[0:n_heads, :]
        o1 = acc_sc[n_heads:2 * n_heads, :] / l_sc[n_heads:2 * n_heads, :]
        o_ref[...] = _sub_rmsnorm(o0 - lam * o1, g_ref, lam_init)


def _diff_sample(q, k_new, v_new, cache_k, cache_v, page_table, brow, lams, sub_g, lam_init):
    s, n_heads, _, dh = q.shape
    n_pages = page_table.shape[1]
    rk, rv = PAGE_SIZE * n_heads * 2, PAGE_SIZE * n_heads
    kern = functools.partial(_diff_sample_kernel, n_seq=s, n_pages=n_pages, n_heads=n_heads, lam_init=lam_init,
                             scale=HEAD_DIM ** -0.5)
    assert n_pages % 2 == 0
    mine = lambda i, p, pt: (i, 0, 0)
    vec = pl.BlockSpec((1, HEAD_DIM), lambda i, p, pt: (0, 0))
    vmem = (4 * RING + 4) * rk * dh * 4 + 8 * rk * dh * 4
    k_buf, v_buf = pltpu.VMEM((RING, rk, dh), F32), pltpu.VMEM((RING, rv, 2 * dh), F32)
    return pl.pallas_call(
        kern,
        grid_spec=pltpu.PrefetchScalarGridSpec(
            num_scalar_prefetch=1, grid=(s, n_pages // 2 + 1),
            in_specs=[pl.BlockSpec((None, 2, n_heads, dh), lambda i, p, pt: (i, 0, 0, 0)),
                      pl.BlockSpec(memory_space=pltpu.HBM),
                      pl.BlockSpec(memory_space=pltpu.HBM),
                      pl.BlockSpec((None, rk, dh), mine),
                      pl.BlockSpec((None, rv, 2 * dh), mine),
                      pl.BlockSpec((3, 2 * n_heads, rv), lambda i, p, pt: (0, 0, 0)),
                      vec, vec, vec, vec,
                      pl.BlockSpec((1, 2 * dh), lambda i, p, pt: (0, 0))],
            out_specs=pl.BlockSpec((None, n_heads, 2 * dh), mine),
            scratch_shapes=[pltpu.VMEM((2 * n_heads, 1), F32), pltpu.VMEM((2 * n_heads, 1), F32),
                            pltpu.VMEM((2 * n_heads, 2 * dh), F32),
                            k_buf, k_buf, v_buf, v_buf,
                            pltpu.SemaphoreType.DMA((4, RING))]),
        out_shape=jax.ShapeDtypeStruct((s, n_heads, 2 * dh), F32),
        compiler_params=_params(("arbitrary", "arbitrary"), vmem),
        name="diff_sample",
    )(page_table, q.transpose(0, 2, 1, 3).astype(BF16), cache_k, cache_v,
      _first_rows(k_new, rk), _first_rows(v_new, rv), brow, *lams, sub_g.reshape(1, 2 * dh))


def _sb_sample_kernel(pt_ref, q_ref, kc_hbm, vc_hbm, tri_ref, o_ref, c_sc, acc_sc, kbuf0, kbuf1, vbuf0, vbuf1, sems,
                      *, n_seq, n_pages, n_heads, scale):
    i = pl.program_id(0)
    p = pl.program_id(1)
    rows = PAGE_SIZE * n_heads
    n_chunks = rows // LANES
    n_steps = n_pages // SB_PAGES

    @pl.when(p == 0)
    def _():
        c_sc[...] = jnp.zeros_like(c_sc)
        acc_sc[...] = jnp.zeros_like(acc_sc)

    def row_starts(u):
        seq, newest = u // n_steps, n_pages - 1 - SB_PAGES * (u % n_steps)
        starts = tuple(pl.multiple_of(pt_ref[seq, newest - j] * rows, rows) for j in range(SB_PAGES))
        return starts + starts

    views = _page_ring(i * n_steps + p, n_seq * n_steps, row_starts, (kc_hbm, kc_hbm, vc_hbm, vc_hbm),
                       (kbuf0, kbuf1, vbuf0, vbuf1), sems)
    row = lax.broadcasted_iota(jnp.int32, (n_heads, LANES), 0)
    valid = _head_match((n_heads, LANES), n_heads, row)[None]

    def page(kc_ref, vc_ref):
        z = lax.dot_general(q_ref[...], kc_ref[...].astype(BF16), NT_DIMS, preferred_element_type=F32) * scale
        z3 = jnp.stack([z[:, j * LANES:(j + 1) * LANES] for j in range(n_chunks)])
        log_beta, log_keep = _sb_logs(z3, valid)
        after = _sum_after(log_keep.reshape(n_chunks * n_heads, LANES),
                           tri_ref[...]).reshape(n_chunks, n_heads, LANES)
        total = after[:, :, 0:1] + log_keep[:, :, 0:1]
        run = c_sc[...]
        later = [None] * n_chunks
        for j in reversed(range(n_chunks)):
            later[j] = run
            run = run + total[j]
        c_sc[...] = run
        a3 = jnp.where(valid, jnp.exp(log_beta + (after + jnp.stack(later))), 0.0)
        a = jnp.concatenate([a3[j] for j in range(n_chunks)], axis=1).astype(BF16)
        acc_sc[...] += jnp.dot(a, vc_ref[...].astype(BF16), preferred_element_type=F32)

    for j in range(SB_PAGES):
        page(views[j], views[SB_PAGES + j])

    @pl.when(p == n_steps - 1)
    def _():
        o_ref[...] = acc_sc[...]


def _sb_sample(q, cache_k, cache_v, page_table):
    s, n_heads, dh = q.shape
    n_pages = page_table.shape[1]
    rows = PAGE_SIZE * n_heads
    assert LANES % n_heads == 0 and SB_PAGES == 2 and n_pages % SB_PAGES == 0
    kern = functools.partial(_sb_sample_kernel, n_seq=s, n_pages=n_pages, n_heads=n_heads, scale=HEAD_DIM ** -0.5)
    mine = lambda i, p, pt: (i, 0, 0)
    vmem = 2 * SB_PAGES * RING * rows * dh * 4 + 4 * rows * dh * 2 + 16 * n_heads * rows * 4
    page_buf = pltpu.VMEM((RING, rows, dh), F32)
    return pl.pallas_call(
        kern,
        grid_spec=pltpu.PrefetchScalarGridSpec(
            num_scalar_prefetch=1, grid=(s, n_pages // SB_PAGES),
            in_specs=[pl.BlockSpec((None, n_heads, dh), mine),
                      pl.BlockSpec(memory_space=pltpu.HBM),
                      pl.BlockSpec(memory_space=pltpu.HBM),
                      pl.BlockSpec((LANES, LANES), lambda i, p, pt: (0, 0))],
            out_specs=pl.BlockSpec((None, n_heads, dh), mine),
            scratch_shapes=[pltpu.VMEM((n_heads, 1), F32), pltpu.VMEM((n_heads, dh), F32),
                            page_buf, page_buf, page_buf, page_buf,
                            pltpu.SemaphoreType.DMA((2 * SB_PAGES, RING))]),
        out_shape=jax.ShapeDtypeStruct((s, n_heads, dh), F32),
        compiler_params=_params(("arbitrary", "arbitrary"), vmem),
        name="sb_sample",
    )(page_table, q.astype(BF16), cache_k, cache_v, _tri_strict(LANES))


def _moba_kmean_kernel(pt_ref, k_hbm, o_ref, abuf, bbuf, sems, *, n_seq, n_blocks, n_heads):
    rows = PAGE_SIZE * n_heads
    n = pl.program_id(1)

    def row_starts(u):
        i, blk = u // n_blocks, u % n_blocks
        return (pl.multiple_of(pt_ref[i, 2 * blk] * rows, rows), pl.multiple_of(pt_ref[i, 2 * blk + 1] * rows, rows))

    ka_ref, kb_ref = _page_ring(pl.program_id(0) * n_blocks + n, n_seq * n_blocks, row_starts, (k_hbm, k_hbm),
                                (abuf, bbuf), sems)
    shape = (PAGE_SIZE, n_heads, HEAD_DIM)
    tot = jnp.sum(ka_ref[...].reshape(shape), axis=0) + jnp.sum(kb_ref[...].reshape(shape), axis=0)
    o_ref[n] = tot * (1.0 / MOBA_BLOCK)


def _moba_kmean(cache_k, page_table, n_heads):
    assert MOBA_BLOCK == 2 * PAGE_SIZE and n_heads % 8 == 0
    s, n_pages = page_table.shape
    assert n_pages % 2 == 0
    nblk = n_pages // 2
    rows = PAGE_SIZE * n_heads
    return pl.pallas_call(
        functools.partial(_moba_kmean_kernel, n_seq=s, n_blocks=nblk, n_heads=n_heads),
        grid_spec=pltpu.PrefetchScalarGridSpec(
            num_scalar_prefetch=1, grid=(s, nblk),
            in_specs=[pl.BlockSpec(memory_space=pltpu.HBM)],
            out_specs=pl.BlockSpec((None, nblk, n_heads, HEAD_DIM), lambda i, n, pt: (i, 0, 0, 0)),
            scratch_shapes=[pltpu.VMEM((RING, rows, HEAD_DIM), F32), pltpu.VMEM((RING, rows, HEAD_DIM), F32),
                            pltpu.SemaphoreType.DMA((2, RING))]),
        out_shape=jax.ShapeDtypeStruct((s, nblk, n_heads, HEAD_DIM), F32),
        compiler_params=_params(("arbitrary", "arbitrary"), (2 * RING + 4) * rows * HEAD_DIM * 4),
        name="moba_kmean",
    )(page_table, cache_k)


def _moba_top_kernel(km_ref, q_ref, o_ref):
    nblk = km_ref.shape[0]
    gates = jnp.sum(km_ref[...] * q_ref[...][None], axis=2, keepdims=True)
    blk = lax.broadcasted_iota(jnp.int32, gates.shape, 0).astype(F32)
    for r in range(MOBA_TOPK):
        best = jnp.max(gates, axis=0, keepdims=True)
        idx = jnp.min(jnp.where(gates == best, blk, float(nblk)), axis=0, keepdims=True)
        o_ref[r] = idx[0].astype(jnp.int32)
        gates = jnp.where(blk == idx, -jnp.inf, gates)


def _moba_top(kmean, q):
    s, nblk, n_heads, dh = kmean.shape
    assert nblk >= MOBA_TOPK
    return pl.pallas_call(
        _moba_top_kernel,
        grid=(s,),
        in_specs=[pl.BlockSpec((None, nblk, n_heads, dh), lambda i: (i, 0, 0, 0)),
                  pl.BlockSpec((None, n_heads, dh), lambda i: (i, 0, 0))],
        out_specs=pl.BlockSpec((None, MOBA_TOPK, n_heads, 1), lambda i: (i, 0, 0, 0)),
        out_shape=jax.ShapeDtypeStruct((s, MOBA_TOPK, n_heads, 1), jnp.int32),
        compiler_params=_params(("arbitrary",), 6 * nblk * n_heads * dh * 4),
        name="moba_top",
    )(kmean, q)


def _moba_attend_kernel(phys_ref, logi_ref, q_ref, kc_hbm, vc_hbm, kn_ref, vn_ref, brow_ref, o_ref,
                        m_sc, l_sc, acc_sc, kbuf0, kbuf1, vbuf0, vbuf1, sems,
                        *, n_seq, n_sel, n_pages, n_heads, scale):
    h = pl.program_id(1)
    j = pl.program_id(2)
    rows = PAGE_SIZE * n_heads
    t = (pl.program_id(0) * n_heads + h) * n_sel + j

    @pl.when(j == 0)
    def _():
        _softmax_init(m_sc, l_sc, acc_sc)

    def process(k_ref, v_ref, kind):
        s = lax.dot_general(q_ref[...], k_ref[...].astype(BF16), NT_DIMS, preferred_element_type=F32)
        s = s * scale + brow_ref[pl.ds(kind, 1), :]
        s = jnp.where(_head_match(s.shape, n_heads, h), s, NEG)
        _softmax_step(s, v_ref[...].astype(BF16), m_sc, l_sc, acc_sc)

    def row_starts(u):
        starts = tuple(pl.multiple_of(phys_ref[2 * u + e] * rows, rows) for e in range(2))
        return starts + starts

    @pl.when(j < n_sel)
    def _():
        views = _page_ring(t, n_seq * n_heads * n_sel, row_starts, (kc_hbm, kc_hbm, vc_hbm, vc_hbm),
                           (kbuf0, kbuf1, vbuf0, vbuf1), sems)
        for e in range(2):
            process(views[e], views[2 + e], jnp.where(logi_ref[2 * t + e] == n_pages - 1, 1, 0))

    @pl.when(j == n_sel)
    def _():
        process(kn_ref, vn_ref, 2)
        o_ref[...] = acc_sc[...] / l_sc[...]


def _moba_attend(q, k_new, v_new, cache_k, cache_v, phys, logical, brow, n_pages):
    s, n_heads, dh = q.shape
    n_sel = phys.shape[2] // 2
    rows = PAGE_SIZE * n_heads
    page_buf = pltpu.VMEM((RING, rows, dh), F32)
    q8 = jnp.zeros((s, n_heads, 8, dh), BF16).at[:, :, 0, :].set(q.astype(BF16))
    kern = functools.partial(_moba_attend_kernel, n_seq=s, n_sel=n_sel, n_pages=n_pages, n_heads=n_heads,
                             scale=HEAD_DIM ** -0.5)
    mine = lambda i, h, j, ph, lg: (i, 0, 0)
    out = pl.pallas_call(
        kern,
        grid_spec=pltpu.PrefetchScalarGridSpec(
            num_scalar_prefetch=2, grid=(s, n_heads, n_sel + 1),
            in_specs=[pl.BlockSpec((None, None, 8, dh), lambda i, h, j, ph, lg: (i, h, 0, 0)),
                      pl.BlockSpec(memory_space=pltpu.HBM),
                      pl.BlockSpec(memory_space=pltpu.HBM),
                      pl.BlockSpec((None, rows, dh), mine),
                      pl.BlockSpec((None, rows, dh), mine),
                      pl.BlockSpec((None, 8, rows), lambda i, h, j, ph, lg: (h, 0, 0))],
            out_specs=pl.BlockSpec((None, None, 8, dh), lambda i, h, j, ph, lg: (i, h, 0, 0)),
            scratch_shapes=[pltpu.VMEM((8, 1), F32), pltpu.VMEM((8, 1), F32), pltpu.VMEM((8, dh), F32),
                            page_buf, page_buf, page_buf, page_buf,
                            pltpu.SemaphoreType.DMA((4, RING))]),
        out_shape=jax.ShapeDtypeStruct((s, n_heads, 8, dh), F32),
        compiler_params=_params(("arbitrary", "arbitrary", "arbitrary"), (4 * RING + 10) * rows * dh * 4),
        name="moba_attend",
    )(phys.reshape(-1), logical.reshape(-1), q8, cache_k, cache_v, _first_rows(k_new, rows),
      _first_rows(v_new, rows), brow)
    return out[:, :, 0, :]


def _moba_sample(q, k_new, v_new, cache_k, cache_v, page_table, brow):
    s, n_heads, dh = q.shape
    n_pages = page_table.shape[1]
    kmean = _moba_kmean(cache_k, page_table, n_heads)
    blocks = _moba_top(kmean, q)[..., 0].transpose(0, 2, 1)
    logical = (2 * blocks[..., None] + jnp.arange(2, dtype=jnp.int32)).reshape(s, n_heads, 2 * MOBA_TOPK)
    phys = jnp.take_along_axis(page_table[:, None, :], logical, axis=2)
    return _moba_attend(q, k_new, v_new, cache_k, cache_v, phys, logical, brow, n_pages)


MM_PROMPT = dict(tm=1024, tn=512, tk=4096)
MM_PROMPT_DOWN = dict(tm=1024, tn=512, tk=5504)
MM_SAMPLE = dict(tm=16, tn=512, tk=16384)


def _lam_init(layer):
    return 0.8 - 0.6 * math.exp(-0.3 * layer)


def _ffn_and_ple(h, p_rows, up_gate, i, wts, mm, mm_down):
    f = _rmsnorm(h, wts['g_ffn'][i], BF16)
    act, g_rows = up_gate(f, wts['w_up'])
    h = _mm(act, wts['w_down'][i], extras=(h,), epilogue=_add_residual, name="down", **mm_down)
    pp = _mm(p_rows[i], wts['w_ple_proj'][i], name="ple_proj", **mm)
    a = _rmsnorm(h, wts['g_ple'][i], BF16)
    h = _mm(a, wts['w_ple_gate'][i], extras=(h, pp), epilogue=_ple_combine, name="ple_gate", **mm)
    return h, g_rows


def kernel(x_prompt, x_sample, cache_moba_k, cache_moba_v, cache_diff_k, cache_diff_v, cache_sb_k, cache_sb_v,
           state_conv, page_table, p_prompt, p_sample, rel_bias, w_in_even, w_out_even, lam_q1, lam_k1, lam_q2,
           lam_k2, diff_subln, w_in_odd, w_out_odd, g_mix, g_ffn, w_up, conv_w, conv_b, w_down, g_ple,
           w_ple_gate, w_ple_proj, g_final):
    batch, seq, d = x_prompt.shape
    dec = x_sample.shape[0]
    depth = g_mix.shape[0]
    assert depth == 2 and x_sample.shape[1] == 1
    dh = HEAD_DIM
    moba_heads = cache_moba_k.shape[3]
    diff_heads = cache_diff_k.shape[3]
    sb_heads = cache_sb_k.shape[3]
    moba_w, diff_w, sb_w = moba_heads * dh, diff_heads * 2 * dh, sb_heads * dh
    dff = conv_w.shape[2]
    n_pages = page_table.shape[1]
    m_p = batch * seq
    m_s = BF16_SUBLANES

    per_layer = lambda w: [w[i].astype(BF16) for i in range(depth)]
    wts = dict(g_ffn=g_ffn, g_ple=g_ple, w_up=w_up, w_down=per_layer(w_down),
               w_ple_gate=per_layer(w_ple_gate), w_ple_proj=per_layer(w_ple_proj))
    cuts_e = np.cumsum([0, moba_w, moba_w, moba_w, diff_w, diff_w, diff_w])
    w_in_e = [w_in_even[0][:, lo:hi].astype(BF16) for lo, hi in zip(cuts_e[:-1], cuts_e[1:])]
    w_in_o = [w_in_odd[0][:, j * sb_w:(j + 1) * sb_w].astype(BF16) for j in range(3)]
    w_out_e_m = w_out_even[0][:moba_w].astype(BF16)
    w_out_e_d = w_out_even[0][moba_w:].astype(BF16)
    w_out_o = w_out_odd[0].astype(BF16)
    lams = tuple(v[0].reshape(1, dh) for v in (lam_q1, lam_k1, lam_q2, lam_k2))
    sub_g = diff_subln[0]
    lam0 = _lam_init(0)

    rb_flat = rel_bias.T.reshape(-1)
    bias_tab = _prompt_bias_tables(rb_flat)
    bias_rows = _sample_bias_rows(rb_flat)

    h = x_prompt.reshape(m_p, d)
    pp_rows = p_prompt.reshape(depth, m_p, -1).astype(BF16)
    tail_tm = 1024
    up_gate_p = lambda i: (lambda f, w: _up_gate_prompt(f, w, i, conv_w[i], conv_b[i], seq, tm=tail_tm))

    a = _rmsnorm(h, g_mix[0], BF16)
    mq, mk, mv, dq, dk, dv = [_mm(a, w, name="in_proj", **MM_PROMPT) for w in w_in_e]
    mo = _moba_prompt(mq, mk, mv, bias_tab, rb_flat, batch, seq)
    do = _diff_prompt(dq, dk, dv, bias_tab, rb_flat, lams, sub_g, batch, seq, head0=moba_heads, lam_init=lam0)
    h = _mm(mo, w_out_e_m, extras=(h,), epilogue=_add_residual, name="out_proj", **MM_PROMPT)
    h = _mm(do, w_out_e_d, extras=(h,), epilogue=_add_residual, name="out_proj", **MM_PROMPT)
    h, tail0 = _ffn_and_ple(h, pp_rows, up_gate_p(0), 0, wts, MM_PROMPT, MM_PROMPT_DOWN)

    a = _rmsnorm(h, g_mix[1], BF16)
    sq, sk, sv = [_mm(a, w, name="in_proj", **MM_PROMPT) for w in w_in_o]
    so = _sb_prompt(sq, sk, sv, batch, seq)
    h = _mm(so, w_out_o, extras=(h,), epilogue=_add_residual, name="out_proj", **MM_PROMPT)
    h, tail1 = _ffn_and_ple(h, pp_rows, up_gate_p(1), 1, wts, MM_PROMPT, MM_PROMPT_DOWN)
    y_prompt = _rmsnorm(h, g_final, F32).reshape(batch, seq, d)

    def conv_rows(tail):
        per_seq = tail.reshape(batch, seq // tail_tm, 8, dff)
        return per_seq[:, -1, 8 - (CONV_W - 1):, :]

    conv_p = jnp.stack([conv_rows(tail0), conv_rows(tail1)])
    rows5 = lambda x, *shape: x.reshape((1, batch, seq) + shape)

    pad = lambda r: jnp.zeros((m_s,) + r.shape[1:], r.dtype).at[:dec].set(r)
    hs = pad(x_sample.reshape(dec, d))
    ps_rows = jnp.stack([pad(p_sample[i].reshape(dec, -1)) for i in range(depth)]).astype(BF16)

    def up_gate_s(i):
        s0, s1 = pad(state_conv[i, :, 0]), pad(state_conv[i, :, 1])
        return lambda f, w: _up_gate_sample(f, w, i, s0, s1, conv_w[i], conv_b[i])

    a = _rmsnorm(hs, g_mix[0], BF16)
    mq_s, mk_s, mv_s, dq_s, dk_s, dv_s = [_mm(a, w, name="in_proj_s", **MM_SAMPLE)[:dec] for w in w_in_e]
    brow_m = jnp.repeat(bias_rows[:moba_heads], moba_heads, axis=2)
    per_map = bias_rows[moba_heads:, :3].transpose(1, 0, 2)
    brow_d = jnp.repeat(jnp.concatenate([per_map, per_map], axis=1), diff_heads, axis=2)
    heads3 = lambda x, n: x.reshape(dec, n, -1)
    mo_s = _moba_sample(heads3(mq_s, moba_heads), heads3(mk_s, moba_heads), heads3(mv_s, moba_heads),
                        cache_moba_k[0].reshape(-1, dh), cache_moba_v[0].reshape(-1, dh), page_table, brow_m)
    do_s = _diff_sample(dq_s.reshape(dec, diff_heads, 2, dh), heads3(dk_s, 2 * diff_heads), heads3(dv_s, diff_heads),
                        cache_diff_k[0].reshape(-1, dh), cache_diff_v[0].reshape(-1, 2 * dh), page_table, brow_d,
                        lams, sub_g, lam0)
    mo_s, do_s = mo_s.reshape(dec, moba_w), do_s.reshape(dec, diff_w)
    hs = _mm(pad(mo_s).astype(BF16), w_out_e_m, extras=(hs,), epilogue=_add_residual, name="out_proj_s", **MM_SAMPLE)
    hs = _mm(pad(do_s).astype(BF16), w_out_e_d, extras=(hs,), epilogue=_add_residual, name="out_proj_s", **MM_SAMPLE)
    hs, gs0 = _ffn_and_ple(hs, ps_rows, up_gate_s(0), 0, wts, MM_SAMPLE, MM_SAMPLE)

    a = _rmsnorm(hs, g_mix[1], BF16)
    sq_s, sk_s, sv_s = [_mm(a, w, name="in_proj_s", **MM_SAMPLE)[:dec] for w in w_in_o]
    so_s = _sb_sample(sq_s.reshape(dec, sb_heads, dh), cache_sb_k[0].reshape(-1, dh),
                      cache_sb_v[0].reshape(-1, dh), page_table)
    hs = _mm(pad(so_s.reshape(dec, sb_w)).astype(BF16), w_out_o, extras=(hs,), epilogue=_add_residual,
             name="out_proj_s", **MM_SAMPLE)
    hs, gs1 = _ffn_and_ple(hs, ps_rows, up_gate_s(1), 1, wts, MM_SAMPLE, MM_SAMPLE)
    y_sample = _rmsnorm(hs, g_final, F32)[:dec].reshape(dec, 1, d)

    conv_s = jnp.stack([jnp.stack([state_conv[i, :, 1], g[:dec]], axis=1) for i, g in enumerate((gs0, gs1))])

    return (y_prompt, y_sample,
            rows5(mk, moba_heads, dh), rows5(mv, moba_heads, dh),
            rows5(dk, diff_heads, 2, dh), rows5(dv, diff_heads, 2 * dh),
            rows5(sk, sb_heads, dh), rows5(sv, sb_heads, dh), conv_p,
            mk_s.reshape(1, dec, 1, moba_heads, dh), mv_s.reshape(1, dec, 1, moba_heads, dh),
            dk_s.reshape(1, dec, 1, diff_heads, 2, dh), dv_s.reshape(1, dec, 1, diff_heads, 2 * dh),
            sk_s.reshape(1, dec, 1, sb_heads, dh), sv_s.reshape(1, dec, 1, sb_heads, dh), conv_s)
```
